```python
import jax, jax.numpy as jnp
from jax import lax
import numpy as np

D_MODEL = 2048
BATCH = 2
SEQ = 16384
DEPTH = 4

GRID_W = 64
CTX_LEN = 256
A_HEADS = 8
A_DV = D_MODEL // A_HEADS
A_DK = A_DV // 2
A_WIDTH = A_HEADS * A_DV
A_CHUNK = 64
A_FBIAS_LO = 3.0
A_FBIAS_HI = 6.0
B_HEADS = 16
B_DH = D_MODEL // B_HEADS
B_WIDTH = B_HEADS * B_DH
NA_KH = 8
NA_KW = 16
NA_QB = 16
ROPE_BASE = 10000.0
EPS = 1e-6
IN_NAMES = ('aq', 'ak', 'av', 'ao', 'az', 'ai_f', 'af_f', 'ai_b', 'af_b', 'bq', 'bk', 'bv', 'bz', 'ga', 'gb')
IN_SIZES = (A_HEADS * A_DK, A_HEADS * A_DK, A_WIDTH, A_WIDTH, A_WIDTH, A_HEADS, A_HEADS, A_HEADS, A_HEADS, B_WIDTH, B_WIDTH, B_WIDTH, B_WIDTH, D_MODEL, D_MODEL)
N_IN = 2 * A_HEADS * A_DK + 3 * A_WIDTH + 4 * A_HEADS + 4 * B_WIDTH + 2 * D_MODEL

kernel_name = 'hybrid_mlstm_natten_block'


def _in_offset(name):
    return int(sum(IN_SIZES[:IN_NAMES.index(name)]))


def _split_in(p):
    out = {}
    off = 0
    for name, size in zip(IN_NAMES, IN_SIZES):
        out[name] = p[..., off:off + size]
        off += size
    return out


def _heads(a, h):
    return a.reshape(a.shape[:-1] + (h, a.shape[-1] // h))


def _rms(x, g):
    xf = x.astype(jnp.float32)
    y = xf * lax.rsqrt(jnp.mean(xf * xf, axis=-1, keepdims=True) + EPS)
    return (y * g.astype(jnp.float32)).astype(x.dtype)


def _modulate(x, g, shift, scale):
    return _rms(x, g) * (1 + scale) + shift


def _rope_tables(n_tok):
    t = jnp.arange(n_tok)
    row = (t // GRID_W).astype(jnp.float32)
    col = (t % GRID_W).astype(jnp.float32)
    nf = A_DK // 4
    inv = ROPE_BASE ** (-jnp.arange(nf, dtype=jnp.float32) / nf)
    ar = row[:, None, None] * inv
    ac = col[:, None, None] * inv
    return (jnp.cos(ar), jnp.sin(ar), jnp.cos(ac), jnp.sin(ac))


def _rope(x, tabs):
    cos_r, sin_r, cos_c, sin_c = tabs
    xr, xc = jnp.split(x, 2, axis=-1)

    def rot(u, cs, sn):
        u1, u2 = jnp.split(u, 2, axis=-1)
        return jnp.concatenate([u1 * cs - u2 * sn, u1 * sn + u2 * cs], axis=-1)

    return jnp.concatenate([rot(xr, cos_r, sin_r), rot(xc, cos_c, sin_c)], axis=-1)


def _mlstm_scan(q, k, v, i_pre, f_pre, state, with_out):
    bsz, n_tok, n_heads, _ = k.shape
    nc = n_tok // A_CHUNK

    def chunks(a):
        return jnp.moveaxis(a.reshape((bsz, nc, A_CHUNK) + a.shape[2:]), 1, 0)

    tril = jnp.tril(jnp.ones((A_CHUNK, A_CHUNK), dtype=bool))

    def step(carry, xs):
        C, n, m = carry
        qc, kc, vc, ic, logf = xs
        b = jnp.cumsum(logf, axis=1)
        g = b[:, -1]
        a = g[:, None] - b + ic
        m_new = jnp.maximum(g + m, jnp.max(a, axis=1))
        w_prev = jnp.exp(g + m - m_new)
        w_tok = jnp.exp(a - m_new[:, None])
        C_new = w_prev[..., None, None] * C + jnp.einsum('bsh,bshv,bshd->bhvd', w_tok, vc, kc)
        n_new = w_prev[..., None] * n + jnp.einsum('bsh,bshd->bhd', w_tok, kc)
        h = None
        if with_out:
            log_d = b[:, :, None] - b[:, None] + ic[:, None]
            log_d = jnp.where(tril[None, :, :, None], log_d, -jnp.inf)
            inter = b + m[:, None]
            m_q = jnp.maximum(inter, jnp.max(log_d, axis=2))
            s = jnp.einsum('bjhd,bshd->bjsh', qc, kc) * jnp.exp(log_d - m_q[:, :, None])
            w_int = jnp.exp(inter - m_q)
            num = jnp.einsum('bjsh,bshv->bjhv', s, vc) + w_int[..., None] * jnp.einsum('bhvd,bjhd->bjhv', C, qc)
            den = jnp.sum(s, axis=2) + w_int * jnp.einsum('bhd,bjhd->bjh', n, qc)
            h = num / jnp.maximum(jnp.abs(den), jnp.exp(-m_q))[..., None]
        return (C_new, n_new, m_new), h

    xs = (chunks(q) if with_out else None, chunks(k), chunks(v), chunks(i_pre), chunks(jax.nn.log_sigmoid(f_pre)))
    state, h = lax.scan(step, state, xs)
    if with_out:
        h = jnp.moveaxis(h, 0, 1).reshape(bsz, n_tok, n_heads, -1)
    return h, state


def _rev(a):
    return jnp.flip(a, axis=1)


def _mlstm_out(h, p, g):
    hn = h * lax.rsqrt(jnp.mean(h * h, axis=-1, keepdims=True) + EPS) * g.astype(jnp.float32).reshape(A_HEADS, A_DV)
    y = hn.reshape(h.shape[:2] + (A_WIDTH,)).astype(p['ao'].dtype)
    return y * jax.nn.sigmoid(p['ao']) * jax.nn.silu(p['az'])


def _mlstm_branch(pl, pc, rope_tabs, norm_g, need_ctx):
    f32 = jnp.float32

    def qkv(p, use_rope):
        q = _heads(p['aq'], A_HEADS).astype(f32)
        k = _heads(p['ak'], A_HEADS).astype(f32) * (A_DK ** -0.5)
        v = _heads(p['av'], A_HEADS).astype(f32)
        if use_rope:
            q = _rope(q, rope_tabs)
            k = _rope(k, rope_tabs)
        return q, k, v

    def gates(p):
        return tuple(p[n].astype(f32) for n in ('ai_f', 'af_f', 'ai_b', 'af_b'))

    ql, kl, vl = qkv(pl, True)
    qc, kc, vc = qkv(pc, False)
    il_f, fl_f, il_b, fl_b = gates(pl)
    ic_f, fc_f, ic_b, fc_b = gates(pc)
    bsz = kc.shape[0]
    zero = (jnp.zeros((bsz, A_HEADS, A_DV, A_DK), f32), jnp.zeros((bsz, A_HEADS, A_DK), f32), jnp.zeros((bsz, A_HEADS), f32))
    hc_f, st_f = _mlstm_scan(qc if need_ctx else None, kc, vc, ic_f, fc_f, zero, need_ctx)
    hc_b, st_b = _mlstm_scan(_rev(qc) if need_ctx else None, _rev(kc), _rev(vc), _rev(ic_b), _rev(fc_b), zero, need_ctx)
    hl_f, _ = _mlstm_scan(ql, kl, vl, il_f, fl_f, st_f, True)
    hl_b, _ = _mlstm_scan(_rev(ql), _rev(kl), _rev(vl), _rev(il_b), _rev(fl_b), st_b, True)
    y_lat = _mlstm_out(hl_f + _rev(hl_b), pl, norm_g)
    y_ctx = _mlstm_out(hc_f + _rev(hc_b), pc, norm_g) if need_ctx else None
    return y_lat, y_ctx


def _na_col_tables():
    nblk = GRID_W // NA_QB
    kcw = NA_QB + NA_KW
    j = np.arange(nblk)
    cs = np.clip(j * NA_QB - NA_KW // 2, 0, GRID_W - kcw)
    col_idx = cs[:, None] + np.arange(kcw)[None]
    qcol = j[:, None] * NA_QB + np.arange(NA_QB)[None]
    ws = np.clip(qcol - NA_KW // 2, 0, GRID_W - NA_KW)
    kcol = col_idx[:, None, :]
    mask = (kcol >= ws[..., None]) & (kcol < ws[..., None] + NA_KW)
    dc = np.clip(kcol - qcol[..., None] + NA_KW - 1, 0, 2 * NA_KW - 2)
    return col_idx, mask, dc


def _na_branch(pl, pc, rows, q_g, k_g, rpb, need_ctx):
    f32 = jnp.float32
    scale = B_DH ** -0.5
    ql = _rms(_heads(pl['bq'], B_HEADS), q_g)
    kl = _rms(_heads(pl['bk'], B_HEADS), k_g)
    vl = _heads(pl['bv'], B_HEADS)
    kc = _rms(_heads(pc['bk'], B_HEADS), k_g)
    vc = _heads(pc['bv'], B_HEADS)
    bsz, n_tok = ql.shape[:2]
    kh = min(NA_KH, rows)
    col_idx, col_mask, dc_idx = _na_col_tables()
    nblk, kcw = col_idx.shape
    n_lat = kh * kcw
    mask = np.broadcast_to(col_mask[:, :, None, :], (nblk, NA_QB, kh, kcw)).reshape(nblk, NA_QB, n_lat)
    kg = kl.reshape(bsz, rows, GRID_W, B_HEADS, B_DH)
    vg = vl.reshape(bsz, rows, GRID_W, B_HEADS, B_DH)
    qg = jnp.moveaxis(ql.reshape(bsz, rows, nblk, NA_QB, B_HEADS, B_DH), 1, 0)

    def gather(a, rs):
        blk = lax.dynamic_slice_in_dim(a, rs, kh, axis=1)[:, :, col_idx]
        return blk.transpose(0, 2, 1, 3, 4, 5).reshape(bsz, nblk, n_lat, B_HEADS, B_DH)

    def row_step(args):
        r, q_r = args
        rs = jnp.clip(r - kh // 2, 0, rows - kh)
        k_b = gather(kg, rs)
        v_b = gather(vg, rs)
        dr = rs + jnp.arange(kh) - r + (NA_KH - 1)
        bias = rpb[:, dr][:, :, dc_idx].transpose(0, 2, 3, 1, 4).reshape(B_HEADS, nblk, NA_QB, n_lat)
        s_lat = jnp.einsum('bjuhd,bjkhd->bhjuk', q_r, k_b).astype(f32) * scale + bias.astype(f32)
        s_lat = jnp.where(mask, s_lat, -jnp.inf)
        s_ctx = jnp.einsum('bjuhd,bchd->bhjuc', q_r, kc).astype(f32) * scale
        p = jax.nn.softmax(jnp.concatenate([s_lat, s_ctx], axis=-1), axis=-1).astype(v_b.dtype)
        return (jnp.einsum('bhjuk,bjkhd->bjuhd', p[..., :n_lat], v_b)
                + jnp.einsum('bhjuc,bchd->bjuhd', p[..., n_lat:], vc))

    o = lax.map(row_step, (jnp.arange(rows), qg))
    o = jnp.moveaxis(o, 0, 1).reshape(bsz, n_tok, B_WIDTH)
    y_lat = o * jax.nn.silu(pl['bz'])
    y_ctx = None
    if need_ctx:
        qc = _rms(_heads(pc['bq'], B_HEADS), q_g)
        s = jnp.einsum('bqhd,bkhd->bhqk', qc, kc).astype(f32) * scale
        p = jax.nn.softmax(s, axis=-1).astype(vc.dtype)
        oc = jnp.einsum('bhqk,bkhd->bqhd', p, vc).reshape(bsz, -1, B_WIDTH)
        y_ctx = oc * jax.nn.silu(pc['bz'])
    return y_lat, y_ctx


def _merge(y_a, y_b, p, w_a, w_b, w_o, b_o):
    y = jax.nn.sigmoid(p['ga']) * (y_a @ w_a) + jax.nn.sigmoid(p['gb']) * (y_b @ w_b)
    return y @ w_o + b_o


def setup_inputs(seed: int = 0) -> dict:
    key = jax.random.key(seed)
    ks = jax.random.split(key, 17)

    def nrm(k, shape, s):
        return jax.random.normal(k, shape, jnp.float32) * s

    fbias = jnp.linspace(A_FBIAS_LO, A_FBIAS_HI, A_HEADS, dtype=jnp.float32)
    off_f = _in_offset('af_f')
    off_b = _in_offset('af_b')
    b_in = nrm(ks[8], (DEPTH, N_IN), 0.01)
    b_in = b_in.at[:, off_f:off_f + A_HEADS].add(fbias).at[:, off_b:off_b + A_HEADS].add(fbias)
    return {
        'x': nrm(ks[0], (BATCH, SEQ, D_MODEL), 1.0),
        'c': nrm(ks[1], (BATCH, D_MODEL), 1.0),
        'ctx': nrm(ks[2], (BATCH, CTX_LEN, D_MODEL), 1.0),
        'c_ctx': nrm(ks[3], (D_MODEL,), 1.0),
        'w_mod': nrm(ks[4], (DEPTH, D_MODEL, 3 * D_MODEL), 0.5 * D_MODEL ** -0.5),
        'b_mod': nrm(ks[5], (DEPTH, 3 * D_MODEL), 0.01),
        'norm_g': 1.0 + nrm(ks[6], (DEPTH, D_MODEL), 0.02),
        'w_in': nrm(ks[7], (DEPTH, D_MODEL, N_IN), D_MODEL ** -0.5),
        'b_in': b_in,
        'a_norm_g': 1.0 + nrm(ks[9], (DEPTH, A_WIDTH), 0.02),
        'w_br_a': nrm(ks[10], (DEPTH, A_WIDTH, D_MODEL), A_WIDTH ** -0.5),
        'w_br_b': nrm(ks[11], (DEPTH, B_WIDTH, D_MODEL), B_WIDTH ** -0.5),
        'na_q_g': 1.0 + nrm(ks[12], (DEPTH, B_DH), 0.02),
        'na_k_g': 1.0 + nrm(ks[13], (DEPTH, B_DH), 0.02),
        'na_rpb': nrm(ks[14], (DEPTH, B_HEADS, 2 * NA_KH - 1, 2 * NA_KW - 1), 0.05),
        'w_out': nrm(ks[15], (DEPTH, D_MODEL, D_MODEL), D_MODEL ** -0.5),
        'b_out': nrm(ks[16], (DEPTH, D_MODEL), 0.01),
    }


def reference(x, c, ctx, c_ctx, w_mod, b_mod, norm_g, w_in, b_in, a_norm_g, w_br_a, w_br_b, na_q_g, na_k_g, na_rpb, w_out, b_out):
    n_tok = x.shape[1]
    rows = n_tok // GRID_W
    rope_tabs = _rope_tables(n_tok)
    sc = jax.nn.silu(c)
    sc_ctx = jax.nn.silu(c_ctx)
    for l in range(DEPTH):
        need_ctx = l < DEPTH - 1
        shift, scale, gate = jnp.split(sc @ w_mod[l] + b_mod[l], 3, axis=-1)
        shift_c, scale_c, gate_c = jnp.split(sc_ctx @ w_mod[l] + b_mod[l], 3, axis=-1)
        h_lat = _modulate(x, norm_g[l], shift[:, None], scale[:, None])
        h_ctx = _modulate(ctx, norm_g[l], shift_c, scale_c)
        pl = _split_in(h_lat @ w_in[l] + b_in[l])
        pc = _split_in(h_ctx @ w_in[l] + b_in[l])
        ya_l, ya_c = _mlstm_branch(pl, pc, rope_tabs, a_norm_g[l], need_ctx)
        yb_l, yb_c = _na_branch(pl, pc, rows, na_q_g[l], na_k_g[l], na_rpb[l], need_ctx)
        x = x + gate[:, None] * _merge(ya_l, yb_l, pl, w_br_a[l], w_br_b[l], w_out[l], b_out[l])
        if need_ctx:
            ctx = ctx + gate_c * _merge(ya_c, yb_c, pc, w_br_a[l], w_br_b[l], w_out[l], b_out[l])
    return x
```

```python
import functools

import numpy as np
import jax
import jax.numpy as jnp
from jax import lax
from jax.experimental import pallas as pl
from jax.experimental.pallas import tpu as pltpu

F32 = jnp.float32
BF16 = jnp.bfloat16

GRID_W = 64
A_HEADS = 8
A_DK = 128
A_DV = 256
B_HEADS = 16
B_DH = 128
NA_KH = 8
NA_KW = 16
ROPE_BASE = 10000.0
EPS = 1e-6
IN_NAMES = ('aq', 'ak', 'av', 'ao', 'az', 'ai_f', 'af_f', 'ai_b', 'af_b', 'bq', 'bk', 'bv', 'bz', 'ga', 'gb')

LANES = 128
V7X_VMEM_BYTES = 64 * 1024 * 1024
VMEM_LIMIT_CAP = 56 * 1024 * 1024

MLSTM_CHUNK = 256
NA_QROWS = 2
NA_KROWS = NA_KH + NA_QROWS - 1
NEG_BIG = -1e30


def _vmem_limit(block_bytes, temp_bytes=0):
    need = 2 * block_bytes + temp_bytes + (4 << 20)
    return int(min(max(need, 16 << 20), VMEM_LIMIT_CAP))


def _nbytes(shape, dtype):
    return int(np.prod(shape)) * jnp.dtype(dtype).itemsize


def _params(n_grid, vmem):
    return pltpu.CompilerParams(dimension_semantics=("arbitrary",) * n_grid, vmem_limit_bytes=vmem)


def _mod_kernel(c_ref, w_ref, b_ref, o_ref):
    c = c_ref[...]
    sc = c * (1.0 / (1.0 + jnp.exp(-c)))
    o_ref[...] = jnp.dot(sc.astype(BF16), w_ref[...], preferred_element_type=F32) + b_ref[...]


def _mod_call(c_rows, w, b):
    m, d = c_rows.shape
    n = w.shape[1]
    tn = 1536
    return pl.pallas_call(
        _mod_kernel,
        grid=(n // tn,),
        in_specs=[pl.BlockSpec((m, d), lambda j: (0, 0)),
                  pl.BlockSpec((d, tn), lambda j: (0, j)),
                  pl.BlockSpec((1, tn), lambda j: (0, j))],
        out_specs=pl.BlockSpec((m, tn), lambda j: (0, j)),
        out_shape=jax.ShapeDtypeStruct((m, n), F32),
        compiler_params=_params(1, _vmem_limit(_nbytes((d, tn), BF16))),
        name="mod_dense",
    )(c_rows, w, b)


def _modulate_kernel(x_ref, g_ref, sh_ref, sc_ref, o_ref):
    x = x_ref[0]
    y = x * lax.rsqrt(jnp.mean(x * x, axis=-1, keepdims=True) + EPS) * g_ref[...]
    o_ref[0] = (y * (1.0 + sc_ref[0]) + sh_ref[0]).astype(BF16)


def _modulate_call(x, g, shift, scale, tm):
    b_, t_, d = x.shape
    return pl.pallas_call(
        _modulate_kernel,
        grid=(b_, t_ // tm),
        in_specs=[pl.BlockSpec((1, tm, d), lambda b, i: (b, i, 0)),
                  pl.BlockSpec((1, d), lambda b, i: (0, 0)),
                  pl.BlockSpec((1, 1, d), lambda b, i: (b, 0, 0)),
                  pl.BlockSpec((1, 1, d), lambda b, i: (b, 0, 0))],
        out_specs=pl.BlockSpec((1, tm, d), lambda b, i: (b, i, 0)),
        out_shape=jax.ShapeDtypeStruct((b_, t_, d), BF16),
        compiler_params=_params(2, _vmem_limit(_nbytes((tm, d), F32) + _nbytes((tm, d), BF16),
                                               2 * _nbytes((tm, d), F32))),
        name="modulate",
    )(x, g, shift, scale)


def _proj_kernel(a_ref, w_ref, b_ref, *rest, epilogue):
    acc = jnp.dot(a_ref[0], w_ref[...], preferred_element_type=F32) + b_ref[...]
    epilogue(acc, *rest)


def _proj_call(a, w, bias, epilogue, extras, extra_specs, out_shape, out_spec, *, tm, tn, name):
    b_, t_, k = a.shape
    n = w.shape[1]
    blocks = _nbytes((tm, k), BF16) + _nbytes((k, tn), BF16) + 2 * _nbytes((tm, tn), F32)
    return pl.pallas_call(
        functools.partial(_proj_kernel, epilogue=epilogue),
        grid=(b_, t_ // tm, n // tn),
        in_specs=[pl.BlockSpec((1, tm, k), lambda b, i, j: (b, i, 0)),
                  pl.BlockSpec((k, tn), lambda b, i, j: (0, j)),
                  pl.BlockSpec((1, tn), lambda b, i, j: (0, j))] + list(extra_specs),
        out_specs=out_spec,
        out_shape=out_shape,
        compiler_params=_params(3, _vmem_limit(blocks, 3 * _nbytes((tm, tn), F32))),
        name=name,
    )(a, w, bias, *extras)


def _sigmoid(x):
    return 1.0 / (1.0 + jnp.exp(-x))


def _ep_plain(acc, o_ref):
    o_ref[0] = acc.astype(o_ref.dtype)


def _ep_sigmoid(acc, o_ref):
    o_ref[0] = _sigmoid(acc)


def _ep_rope(acc, tab_ref, o_ref):
    cos = tab_ref[0, 0]
    sin = tab_ref[0, 1]
    for h in range(acc.shape[1] // A_DK):
        x = acc[:, h * A_DK:(h + 1) * A_DK]
        o_ref[0, :, h * A_DK:(h + 1) * A_DK] = (x * cos + pltpu.roll(x, A_DK // 2, 1) * sin).astype(o_ref.dtype)


def _ep_outgate(acc, o_ref):
    half = acc.shape[1] // 2
    ao = acc[:, :half]
    az = acc[:, half:]
    o_ref[0] = _sigmoid(ao) * (az * _sigmoid(az))


def _ep_headnorm(acc, gain_ref, o_ref):
    for h in range(acc.shape[1] // B_DH):
        x = acc[:, h * B_DH:(h + 1) * B_DH]
        y = x * lax.rsqrt(jnp.mean(x * x, axis=-1, keepdims=True) + EPS)
        o_ref[0, h] = (y * gain_ref[:, h * B_DH:(h + 1) * B_DH]).astype(o_ref.dtype)


def _ep_headmajor(acc, o_ref):
    for h in range(acc.shape[1] // B_DH):
        o_ref[0, h] = acc[:, h * B_DH:(h + 1) * B_DH].astype(o_ref.dtype)


def _ep_headmajor_silu(acc, o_ref):
    for h in range(acc.shape[1] // B_DH):
        x = acc[:, h * B_DH:(h + 1) * B_DH]
        o_ref[0, h] = (x * _sigmoid(x)).astype(o_ref.dtype)


def _mlstm_kernel(qf_ref, kf_ref, vf_ref, gf_ref, qb_ref, kb_ref, vb_ref, gb_ref,
                  c0_ref, n0_ref, m0_ref, hf_ref, hb_ref, c_ref, n_ref, m_ref, *, chunk):
    L = chunk

    @pl.when(pl.program_id(1) == 0)
    def _():
        c_ref[...] = c0_ref[...]
        n_ref[...] = n0_ref[...]
        m_ref[...] = m0_ref[...]

    row = lax.broadcasted_iota(jnp.int32, (L, L), 0)
    col = lax.broadcasted_iota(jnp.int32, (L, L), 1)
    lane = lax.broadcasted_iota(jnp.int32, (1, LANES), 1)
    m_all = m_ref[0]
    m_next = m_all
    dirs = ((qf_ref, kf_ref, vf_ref, gf_ref, hf_ref, row >= col),
            (qb_ref, kb_ref, vb_ref, gb_ref, hb_ref, row <= col))
    for d, (q_ref, k_ref, v_ref, g_ref, h_ref, mask) in enumerate(dirs):
        gates = g_ref[0]
        ig = gates[:, :LANES]
        fg = gates[:, LANES:]
        logf = -(jnp.maximum(-fg, 0.0) + jnp.log1p(jnp.exp(-jnp.abs(fg))))
        bcum = jnp.dot(mask.astype(F32), logf, precision=lax.Precision.HIGHEST, preferred_element_type=F32)
        bcum_t = bcum.T
        ig_t = ig.T
        g_tot = bcum[L - 1:L, :] if d == 0 else bcum[0:1, :]
        for h in range(A_HEADS):
            ch = d * A_HEADS + h
            q = q_ref[0, :, h * A_DK:(h + 1) * A_DK]
            k = k_ref[0, :, h * A_DK:(h + 1) * A_DK]
            v = v_ref[0, :, h * A_DV:(h + 1) * A_DV]
            c_st = c_ref[0, ch]
            n_st = n_ref[0, ch]
            m_st = m_all[:, ch:ch + 1]
            b_col = bcum[:, ch:ch + 1]
            i_col = ig[:, ch:ch + 1]
            b_row = bcum_t[ch:ch + 1, :]
            i_row = ig_t[ch:ch + 1, :]
            g = g_tot[:, ch:ch + 1]

            log_d = jnp.where(mask, b_col - b_row + i_row, -jnp.inf)
            inter = b_col + m_st
            m_q = jnp.maximum(inter, jnp.max(log_d, axis=1, keepdims=True))
            s = lax.dot_general(q, k, (((1,), (1,)), ((), ())), preferred_element_type=F32)
            s = s * jnp.exp(log_d - m_q)
            w_int = jnp.exp(inter - m_q)
            num = (jnp.dot(s.astype(BF16), v.astype(BF16), preferred_element_type=F32)
                   + w_int * jnp.dot(q, c_st.astype(BF16), preferred_element_type=F32))
            den = (jnp.sum(s, axis=1, keepdims=True)
                   + w_int * jnp.sum(q.astype(F32) * n_st, axis=1, keepdims=True))
            h_ref[0, :, h * A_DV:(h + 1) * A_DV] = num / jnp.maximum(jnp.abs(den), jnp.exp(-m_q))

            a = g - b_col + i_col
            m_new = jnp.maximum(g + m_st, jnp.max(a, axis=0, keepdims=True))
            w_prev = jnp.exp(g + m_st - m_new)
            w_tok = jnp.exp(a - m_new)
            c_ref[0, ch] = w_prev * c_st + lax.dot_general(
                k, (w_tok * v).astype(BF16), (((0,), (0,)), ((), ())), preferred_element_type=F32)
            n_ref[0, ch] = w_prev * n_st + jnp.sum(w_tok * k.astype(F32), axis=0, keepdims=True)
            m_next = jnp.where(lane == ch, m_new, m_next)
    m_ref[0] = m_next


def _mlstm_call(qk, v, gates, state, *, chunk, name):
    b_, t_, _ = v.shape
    nc = t_ // chunk
    wq = A_HEADS * A_DK
    wv = A_HEADS * A_DV
    c0, n0, m0 = state
    fwd = lambda b, c: c
    bwd = lambda b, c: nc - 1 - c

    def spec(width, blk, which):
        return pl.BlockSpec((1, chunk, width), lambda b, c: (b, which(b, c), blk))

    st_specs = [pl.BlockSpec((1, 2 * A_HEADS, A_DK, A_DV), lambda b, c: (b, 0, 0, 0)),
                pl.BlockSpec((1, 2 * A_HEADS, 1, A_DK), lambda b, c: (b, 0, 0, 0)),
                pl.BlockSpec((1, 1, LANES), lambda b, c: (b, 0, 0))]
    blocks = (4 * _nbytes((chunk, wq), BF16) + 4 * _nbytes((chunk, wv), F32)
              + 2 * _nbytes((chunk, 2 * LANES), F32) + 2 * _nbytes((2 * A_HEADS, A_DK, A_DV), F32))
    return pl.pallas_call(
        functools.partial(_mlstm_kernel, chunk=chunk),
        grid=(b_, nc),
        in_specs=[spec(wq, 0, fwd), spec(wq, 1, fwd), spec(wv, 0, fwd), spec(2 * LANES, 0, fwd),
                  spec(wq, 0, bwd), spec(wq, 1, bwd), spec(wv, 0, bwd), spec(2 * LANES, 0, bwd)] + st_specs,
        out_specs=[spec(wv, 0, fwd), spec(wv, 0, bwd)] + st_specs,
        out_shape=[jax.ShapeDtypeStruct((b_, t_, wv), F32), jax.ShapeDtypeStruct((b_, t_, wv), F32),
                   jax.ShapeDtypeStruct(c0.shape, F32), jax.ShapeDtypeStruct(n0.shape, F32),
                   jax.ShapeDtypeStruct(m0.shape, F32)],
        compiler_params=_params(2, _vmem_limit(blocks, 24 * _nbytes((chunk, chunk), F32))),
        name=name,
    )(qk, qk, v, gates, qk, qk, v, gates, c0, n0, m0)


def _mlstm_out_kernel(hf_ref, hb_ref, og_ref, g_ref, o_ref):
    for h in range(A_HEADS):
        sl = slice(h * A_DV, (h + 1) * A_DV)
        x = hf_ref[0, :, sl] + hb_ref[0, :, sl]
        y = x * lax.rsqrt(jnp.mean(x * x, axis=-1, keepdims=True) + EPS) * g_ref[:, sl]
        o_ref[0, :, sl] = (y * og_ref[0, :, sl]).astype(BF16)


def _mlstm_out_call(hf, hb, og, g, tm):
    b_, t_, w = hf.shape
    spec = pl.BlockSpec((1, tm, w), lambda b, i: (b, i, 0))
    return pl.pallas_call(
        _mlstm_out_kernel,
        grid=(b_, t_ // tm),
        in_specs=[spec, spec, spec, pl.BlockSpec((1, w), lambda b, i: (0, 0))],
        out_specs=spec,
        out_shape=jax.ShapeDtypeStruct((b_, t_, w), BF16),
        compiler_params=_params(2, _vmem_limit(4 * _nbytes((tm, w), F32), 2 * _nbytes((tm, w), F32))),
        name="mlstm_out",
    )(hf, hb, og, g)


def _na_kernel(q_ref, k_ref, v_ref, kc_ref, vc_ref, z_ref, bias_ref, o_ref, *, rows, rows_per_step):
    nq = NA_QROWS * GRID_W
    nk = NA_KROWS * GRID_W
    r0 = pl.program_id(2) * rows_per_step
    kc = kc_ref[0, 0]
    vc = vc_ref[0, 0]

    def body(it, carry):
        r = r0 + it * NA_QROWS
        base = jnp.clip(r - NA_KH // 2, 0, rows - NA_KROWS)
        variant = jnp.where(r == 0, 0, jnp.where(r == 2, 1, jnp.where(r == rows - 4, 3, jnp.where(r == rows - 2, 4, 2))))
        qs = pl.multiple_of(it * nq, nq)
        ks = pl.multiple_of(base * GRID_W, GRID_W)
        q = q_ref[0, 0, pl.ds(qs, nq), :]
        k = k_ref[0, 0, pl.ds(ks, nk), :]
        v = v_ref[0, 0, pl.ds(ks, nk), :]
        s = lax.dot_general(q, k, (((1,), (1,)), ((), ())), preferred_element_type=F32) + bias_ref[0, variant]
        sc = lax.dot_general(q, kc, (((1,), (1,)), ((), ())), preferred_element_type=F32)
        m = jnp.maximum(jnp.max(s, axis=1, keepdims=True), jnp.max(sc, axis=1, keepdims=True))
        p = jnp.exp(s - m)
        pc = jnp.exp(sc - m)
        l = jnp.sum(p, axis=1, keepdims=True) + jnp.sum(pc, axis=1, keepdims=True)
        o = (jnp.dot(p.astype(BF16), v, preferred_element_type=F32)
             + jnp.dot(pc.astype(BF16), vc, preferred_element_type=F32)) / l
        o_ref[0, 0, pl.ds(qs, nq), :] = (o * z_ref[0, 0, pl.ds(qs, nq), :]).astype(BF16)
        return carry

    lax.fori_loop(0, rows_per_step // NA_QROWS, body, 0)


def _na_call(qk, v, qk_ctx, v_ctx, z, bias):
    b_, _, t_, dh = v.shape
    tc = v_ctx.shape[2]
    rows = t_ // GRID_W
    rows_per_step = min(rows, 64)
    tq = rows_per_step * GRID_W
    nvar, nq, nk = bias.shape[1:]
    blocks = (2 * _nbytes((tq, dh), BF16) + 2 * _nbytes((t_, dh), BF16) + 2 * _nbytes((tc, dh), BF16)
              + _nbytes((tq, dh), F32) + _nbytes((nvar, nq, nk), F32))
    return pl.pallas_call(
        functools.partial(_na_kernel, rows=rows, rows_per_step=rows_per_step),
        grid=(b_, B_HEADS, t_ // tq),
        in_specs=[pl.BlockSpec((1, 1, tq, dh), lambda b, h, i: (b, h, i, 0)),
                  pl.BlockSpec((1, 1, t_, dh), lambda b, h, i: (b, B_HEADS + h, 0, 0)),
                  pl.BlockSpec((1, 1, t_, dh), lambda b, h, i: (b, h, 0, 0)),
                  pl.BlockSpec((1, 1, tc, dh), lambda b, h, i: (b, B_HEADS + h, 0, 0)),
                  pl.BlockSpec((1, 1, tc, dh), lambda b, h, i: (b, h, 0, 0)),
                  pl.BlockSpec((1, 1, tq, dh), lambda b, h, i: (b, h, i, 0)),
                  pl.BlockSpec((1, nvar, nq, nk), lambda b, h, i: (h, 0, 0, 0))],
        out_specs=pl.BlockSpec((1, 1, tq, dh), lambda b, h, i: (b, h, i, 0)),
        out_shape=jax.ShapeDtypeStruct((b_, B_HEADS, t_, dh), BF16),
        compiler_params=_params(3, _vmem_limit(blocks, 8 * _nbytes((nq, nk + tc), F32))),
        name="na_attention",
    )(qk, qk, v, qk_ctx, v_ctx, z, bias)


def _ctx_attn_kernel(q_ref, k_ref, v_ref, z_ref, o_ref):
    q = q_ref[0, 0]
    s = lax.dot_general(q, k_ref[0, 0], (((1,), (1,)), ((), ())), preferred_element_type=F32)
    m = jnp.max(s, axis=1, keepdims=True)
    p = jnp.exp(s - m)
    l = jnp.sum(p, axis=1, keepdims=True)
    o = jnp.dot(p.astype(BF16), v_ref[0, 0], preferred_element_type=F32) / l
    o_ref[0, 0] = (o * z_ref[0, 0]).astype(BF16)


def _ctx_attn_call(qk_ctx, v_ctx, z_ctx):
    b_, _, tc, dh = v_ctx.shape
    blk = (1, 1, tc, dh)
    return pl.pallas_call(
        _ctx_attn_kernel,
        grid=(b_, B_HEADS),
        in_specs=[pl.BlockSpec(blk, lambda b, h: (b, h, 0, 0)),
                  pl.BlockSpec(blk, lambda b, h: (b, B_HEADS + h, 0, 0)),
                  pl.BlockSpec(blk, lambda b, h: (b, h, 0, 0)),
                  pl.BlockSpec(blk, lambda b, h: (b, h, 0, 0))],
        out_specs=pl.BlockSpec(blk, lambda b, h: (b, h, 0, 0)),
        out_shape=jax.ShapeDtypeStruct((b_, B_HEADS, tc, dh), BF16),
        compiler_params=_params(2, _vmem_limit(5 * _nbytes((tc, dh), F32), 4 * _nbytes((tc, tc), F32))),
        name="ctx_attention",
    )(qk_ctx, qk_ctx, v_ctx, z_ctx)


def _merge1_kernel(ya_ref, yb_ref, wa_ref, wb_ref, sa_ref, sb_ref, o_ref, ybcat_ref):
    @pl.when(pl.program_id(2) == 0)
    def _():
        for h in range(B_HEADS):
            ybcat_ref[:, h * B_DH:(h + 1) * B_DH] = yb_ref[0, h]

    ua = jnp.dot(ya_ref[0], wa_ref[...], preferred_element_type=F32)
    ub = jnp.dot(ybcat_ref[...], wb_ref[...], preferred_element_type=F32)
    o_ref[0] = (sa_ref[0] * ua + sb_ref[0] * ub).astype(BF16)


def _merge1_call(ya, yb, wa, wb, sgab, *, tm, tn):
    b_, t_, d = ya.shape
    nj = d // tn
    blocks = (2 * _nbytes((tm, d), BF16) + 2 * _nbytes((d, tn), BF16) + 2 * _nbytes((tm, tn), F32)
              + _nbytes((tm, tn), BF16))
    return pl.pallas_call(
        _merge1_kernel,
        grid=(b_, t_ // tm, nj),
        in_specs=[pl.BlockSpec((1, tm, d), lambda b, i, j: (b, i, 0)),
                  pl.BlockSpec((1, B_HEADS, tm, B_DH), lambda b, i, j: (b, 0, i, 0)),
                  pl.BlockSpec((d, tn), lambda b, i, j: (0, j)),
                  pl.BlockSpec((d, tn), lambda b, i, j: (0, j)),
                  pl.BlockSpec((1, tm, tn), lambda b, i, j: (b, i, j)),
                  pl.BlockSpec((1, tm, tn), lambda b, i, j: (b, i, nj + j))],
        out_specs=pl.BlockSpec((1, tm, tn), lambda b, i, j: (b, i, j)),
        out_shape=jax.ShapeDtypeStruct((b_, t_, d), BF16),
        scratch_shapes=[pltpu.VMEM((tm, d), BF16)],
        compiler_params=_params(3, _vmem_limit(blocks, _nbytes((tm, d), BF16) + 3 * _nbytes((tm, tn), F32))),
        name="merge_branches",
    )(ya, yb, wa, wb, sgab, sgab)


def _merge2_kernel(u_ref, w_ref, b_ref, x_ref, g_ref, o_ref):
    y = jnp.dot(u_ref[0], w_ref[...], preferred_element_type=F32) + b_ref[...]
    o_ref[0] = x_ref[0] + g_ref[0] * y


def _merge2_call(u, wo, bo, x, gate, *, tm, tn):
    b_, t_, d = u.shape
    blocks = _nbytes((tm, d), BF16) + _nbytes((d, tn), BF16) + 2 * _nbytes((tm, tn), F32)
    return pl.pallas_call(
        _merge2_kernel,
        grid=(b_, t_ // tm, d // tn),
        in_specs=[pl.BlockSpec((1, tm, d), lambda b, i, j: (b, i, 0)),
                  pl.BlockSpec((d, tn), lambda b, i, j: (0, j)),
                  pl.BlockSpec((1, tn), lambda b, i, j: (0, j)),
                  pl.BlockSpec((1, tm, tn), lambda b, i, j: (b, i, j)),
                  pl.BlockSpec((1, 1, tn), lambda b, i, j: (b, 0, j))],
        out_specs=pl.BlockSpec((1, tm, tn), lambda b, i, j: (b, i, j)),
        out_shape=jax.ShapeDtypeStruct((b_, t_, d), F32),
        compiler_params=_params(3, _vmem_limit(blocks, 2 * _nbytes((tm, tn), F32))),
        name="merge_out",
    )(u, wo, bo, x, gate)


def _in_offsets(d_model):
    a_width = A_HEADS * A_DV
    b_width = B_HEADS * B_DH
    sizes = (A_HEADS * A_DK, A_HEADS * A_DK, a_width, a_width, a_width, A_HEADS, A_HEADS, A_HEADS, A_HEADS,
             b_width, b_width, b_width, b_width, d_model, d_model)
    offs = np.concatenate([[0], np.cumsum(sizes)])
    return {n: (int(offs[i]), int(sizes[i])) for i, n in enumerate(IN_NAMES)}


def _rope_perm():
    q = A_DK // 4
    head = np.concatenate([np.arange(0, q), np.arange(2 * q, 3 * q), np.arange(q, 2 * q), np.arange(3 * q, 4 * q)])
    return np.concatenate([h * A_DK + head for h in range(A_HEADS)])


def _prep_in_weights(w, b, na_q_g, na_k_g):
    offs = _in_offsets(w.shape[0])

    def seg(name):
        o, s = offs[name]
        return w[:, o:o + s], b[o:o + s]

    perm = _rope_perm()
    wq, bq = seg('aq')
    wk, bk = seg('ak')
    out = {}
    out['qk'] = (jnp.concatenate([wq[:, perm], wk[:, perm]], axis=1), jnp.concatenate([bq[perm], bk[perm]]))
    out['av'] = seg('av')
    wo_, bo_ = seg('ao')
    wz_, bz_ = seg('az')
    half = 512
    wt, bt = [], []
    for t in range(wo_.shape[1] // half):
        sl = slice(t * half, (t + 1) * half)
        wt += [wo_[:, sl], wz_[:, sl]]
        bt += [bo_[sl], bz_[sl]]
    out['og'] = (jnp.concatenate(wt, axis=1), jnp.concatenate(bt))
    wg = jnp.zeros((w.shape[0], 2 * LANES), w.dtype)
    bg = jnp.zeros((2 * LANES,), b.dtype)
    for name, lane0 in (('ai_f', 0), ('ai_b', A_HEADS), ('af_f', LANES), ('af_b', LANES + A_HEADS)):
        ws, bs = seg(name)
        wg = wg.at[:, lane0:lane0 + A_HEADS].set(ws)
        bg = bg.at[lane0:lane0 + A_HEADS].set(bs)
    out['gates'] = (wg, bg)
    wbq, bbq = seg('bq')
    wbk, bbk = seg('bk')
    out['bqk'] = (jnp.concatenate([wbq, wbk], axis=1), jnp.concatenate([bbq, bbk]))
    out['bv'] = seg('bv')
    out['bz'] = seg('bz')
    wga, bga = seg('ga')
    wgb, bgb = seg('gb')
    out['gab'] = (jnp.concatenate([wga, wgb], axis=1), jnp.concatenate([bga, bgb]))
    out = {k: (wv.astype(BF16), bv.reshape(1, -1).astype(F32)) for k, (wv, bv) in out.items()}
    scale = B_DH ** -0.5
    out['bqk_gain'] = jnp.concatenate([jnp.tile(na_q_g * scale, B_HEADS), jnp.tile(na_k_g, B_HEADS)]).reshape(1, -1).astype(F32)
    return out


def _rope_tables(n_tok, with_rope):
    nf = A_DK // 4
    kscale = A_DK ** -0.5
    if with_rope:
        t = jnp.arange(n_tok)
        rowp = (t // GRID_W).astype(F32)
        colp = (t % GRID_W).astype(F32)
        inv = ROPE_BASE ** (-jnp.arange(nf, dtype=F32) / nf)
        ar = rowp[:, None] * inv
        ac = colp[:, None] * inv
        cos = jnp.concatenate([jnp.cos(ar), jnp.cos(ac), jnp.cos(ar), jnp.cos(ac)], axis=1)
        sin = jnp.concatenate([-jnp.sin(ar), -jnp.sin(ac), jnp.sin(ar), jnp.sin(ac)], axis=1)
    else:
        cos = jnp.ones((n_tok, A_DK), F32)
        sin = jnp.zeros((n_tok, A_DK), F32)
    tq = jnp.stack([cos, sin])
    return jnp.stack([tq, tq * kscale])


def _na_bias_tables(rpb, rows):
    assert rows >= 2 * NA_KH, "neighbourhood attention blocks assume at least 16 grid rows"
    tabs = []
    for r in (0, 2, 4, rows - 4, rows - 2):
        base = int(np.clip(r - NA_KH // 2, 0, rows - NA_KROWS))
        qr = r + np.arange(NA_QROWS)
        rs = np.clip(qr - NA_KH // 2, 0, rows - NA_KH)
        kr = base + np.arange(NA_KROWS)
        rvalid = (kr[None, :] >= rs[:, None]) & (kr[None, :] < rs[:, None] + NA_KH)
        dr = np.clip(kr[None, :] - qr[:, None] + NA_KH - 1, 0, 2 * NA_KH - 2)
        qc = np.arange(GRID_W)
        kc = np.arange(GRID_W)
        ws = np.clip(qc - NA_KW // 2, 0, GRID_W - NA_KW)
        cvalid = (kc[None, :] >= ws[:, None]) & (kc[None, :] < ws[:, None] + NA_KW)
        dc = np.clip(kc[None, :] - qc[:, None] + NA_KW - 1, 0, 2 * NA_KW - 2)
        shape = (NA_QROWS, GRID_W, NA_KROWS, GRID_W)
        dr_f = np.broadcast_to(dr[:, None, :, None], shape)
        dc_f = np.broadcast_to(dc[None, :, None, :], shape)
        valid = np.broadcast_to(rvalid[:, None, :, None] & cvalid[None, :, None, :], shape)
        bias = rpb[:, dr_f, dc_f]
        bias = jnp.where(valid[None], bias, NEG_BIG)
        tabs.append(bias.reshape(rpb.shape[0], NA_QROWS * GRID_W, NA_KROWS * GRID_W))
    return jnp.stack(tabs, axis=1).astype(F32)


def _project(h, wts, rope_tab, *, tm):
    b_, t_, d = h.shape
    tn = 1024
    nt = t_ // tm
    res = {}
    res['qk'] = _proj_call(
        h, *wts['qk'], _ep_rope, [rope_tab],
        [pl.BlockSpec((1, 2, tm, A_DK), lambda b, i, j: (j, 0, i, 0))],
        jax.ShapeDtypeStruct((b_, t_, 2 * A_HEADS * A_DK), BF16),
        pl.BlockSpec((1, tm, tn), lambda b, i, j: (b, i, j)), tm=tm, tn=tn, name="proj_qk_rope")
    res['av'] = _proj_call(
        h, *wts['av'], _ep_plain, [], [],
        jax.ShapeDtypeStruct((b_, t_, A_HEADS * A_DV), F32),
        pl.BlockSpec((1, tm, tn), lambda b, i, j: (b, i, j)), tm=tm, tn=tn, name="proj_av")
    res['og'] = _proj_call(
        h, *wts['og'], _ep_outgate, [], [],
        jax.ShapeDtypeStruct((b_, t_, A_HEADS * A_DV), F32),
        pl.BlockSpec((1, tm, tn // 2), lambda b, i, j: (b, i, j)), tm=tm, tn=tn, name="proj_outgate")
    res['gates'] = _proj_call(
        h, *wts['gates'], _ep_plain, [], [],
        jax.ShapeDtypeStruct((b_, t_, 2 * LANES), F32),
        pl.BlockSpec((1, tm, 2 * LANES), lambda b, i, j: (b, i, j)), tm=tm, tn=2 * LANES, name="proj_gates")
    hpt = tn // B_DH
    hm_spec = pl.BlockSpec((1, hpt, tm, B_DH), lambda b, i, j: (b, j, i, 0))
    res['bqk'] = _proj_call(
        h, *wts['bqk'], _ep_headnorm, [wts['bqk_gain']],
        [pl.BlockSpec((1, tn), lambda b, i, j: (0, j))],
        jax.ShapeDtypeStruct((b_, 2 * B_HEADS, t_, B_DH), BF16), hm_spec, tm=tm, tn=tn, name="proj_bqk_norm")
    res['bv'] = _proj_call(
        h, *wts['bv'], _ep_headmajor, [], [],
        jax.ShapeDtypeStruct((b_, B_HEADS, t_, B_DH), BF16), hm_spec, tm=tm, tn=tn, name="proj_bv")
    res['bz'] = _proj_call(
        h, *wts['bz'], _ep_headmajor_silu, [], [],
        jax.ShapeDtypeStruct((b_, B_HEADS, t_, B_DH), F32), hm_spec, tm=tm, tn=tn, name="proj_bz_silu")
    res['gab'] = _proj_call(
        h, *wts['gab'], _ep_sigmoid, [], [],
        jax.ShapeDtypeStruct((b_, t_, 2 * d), F32),
        pl.BlockSpec((1, tm, tn), lambda b, i, j: (b, i, j)), tm=tm, tn=tn, name="proj_gab_sigmoid")
    return res


def _merge(ya, yb, sgab, wa, wb, wo, bo, x, gate, *, tm):
    u = _merge1_call(ya, yb, wa, wb, sgab, tm=tm, tn=512)
    return _merge2_call(u, wo, bo, x, gate, tm=tm, tn=1024)


def kernel(x, c, ctx, c_ctx, w_mod, b_mod, norm_g, w_in, b_in, a_norm_g, w_br_a, w_br_b, na_q_g, na_k_g, na_rpb, w_out, b_out):
    bsz, n_tok, d = x.shape
    n_ctx = ctx.shape[1]
    depth = w_mod.shape[0]
    rows = n_tok // GRID_W
    assert n_tok % MLSTM_CHUNK == 0 and n_ctx % MLSTM_CHUNK == 0 and n_tok % (NA_QROWS * GRID_W) == 0
    mod_rows = 16
    assert bsz + 1 <= mod_rows
    tm_lat = min(1024, n_tok)
    tm_ctx = min(1024, n_ctx)
    te_lat = min(512, n_tok)
    te_ctx = min(512, n_ctx)

    rope_lat = _rope_tables(n_tok, True)
    rope_ctx = _rope_tables(n_ctx, False)
    c_rows = jnp.zeros((mod_rows, d), F32).at[:bsz].set(c).at[bsz].set(c_ctx)
    zero_state = (jnp.zeros((bsz, 2 * A_HEADS, A_DK, A_DV), F32),
                  jnp.zeros((bsz, 2 * A_HEADS, 1, A_DK), F32),
                  jnp.zeros((bsz, 1, LANES), F32))

    for l in range(depth):
        need_ctx = l < depth - 1
        mod = _mod_call(c_rows, w_mod[l].astype(BF16), b_mod[l].reshape(1, -1))
        shift, scale, gate = (mod[:bsz, k * d:(k + 1) * d].reshape(bsz, 1, d) for k in range(3))
        shift_c, scale_c, gate_c = (jnp.broadcast_to(mod[bsz, k * d:(k + 1) * d].reshape(1, 1, d), (bsz, 1, d))
                                    for k in range(3))
        g_row = norm_g[l].reshape(1, d)
        h_lat = _modulate_call(x, g_row, shift, scale, te_lat)
        h_ctx = _modulate_call(ctx, g_row, shift_c, scale_c, te_ctx)

        wts = _prep_in_weights(w_in[l], b_in[l], na_q_g[l], na_k_g[l])
        pl_ = _project(h_lat, wts, rope_lat, tm=tm_lat)
        pc_ = _project(h_ctx, wts, rope_ctx, tm=tm_ctx)

        hcf, hcb, c1, n1, m1 = _mlstm_call(pc_['qk'], pc_['av'], pc_['gates'], zero_state,
                                           chunk=MLSTM_CHUNK, name="mlstm_ctx")
        hlf, hlb, _, _, _ = _mlstm_call(pl_['qk'], pl_['av'], pl_['gates'], (c1, n1, m1),
                                        chunk=MLSTM_CHUNK, name="mlstm_lat")
        a_g = a_norm_g[l].reshape(1, -1)
        ya_l = _mlstm_out_call(hlf, hlb, pl_['og'], a_g, te_lat)

        bias = _na_bias_tables(na_rpb[l], rows)
        yb_l = _na_call(pl_['bqk'], pl_['bv'], pc_['bqk'], pc_['bv'], pl_['bz'], bias)

        wa = w_br_a[l].astype(BF16)
        wb = w_br_b[l].astype(BF16)
        wo = w_out[l].astype(BF16)
        bo = b_out[l].reshape(1, d)
        x = _merge(ya_l, yb_l, pl_['gab'], wa, wb, wo, bo, x, gate, tm=tm_lat)
        if need_ctx:
            ya_c = _mlstm_out_call(hcf, hcb, pc_['og'], a_g, te_ctx)
            yb_c = _ctx_attn_call(pc_['bqk'], pc_['bv'], pc_['bz'])
            ctx = _merge(ya_c, yb_c, pc_['gab'], wa, wb, wo, bo, ctx, gate_c, tm=tm_ctx)
    return x
```

```python
import functools

import numpy as np
import jax
import jax.numpy as jnp
from jax import lax
from jax.experimental import pallas as pl
from jax.experimental.pallas import tpu as pltpu

F32 = jnp.float32
BF16 = jnp.bfloat16

GRID_W = 64
A_HEADS = 8
A_DK = 128
A_DV = 256
B_HEADS = 16
B_DH = 128
NA_KH = 8
NA_KW = 16
ROPE_BASE = 10000.0
EPS = 1e-6
IN_NAMES = ('aq', 'ak', 'av', 'ao', 'az', 'ai_f', 'af_f', 'ai_b', 'af_b', 'bq', 'bk', 'bv', 'bz', 'ga', 'gb')

LANES = 128
SUBLANES = 8
VMEM_LIMIT_CAP = 56 * 1024 * 1024

MLSTM_CHUNK = 256
NA_QROWS = 2
NA_KROWS = NA_KH + NA_QROWS - 1
NA_UNROLL = 4
NEG_BIG = -1e30


def _vmem_limit(block_bytes, temp_bytes=0):
    need = 2 * block_bytes + temp_bytes + (4 << 20)
    return int(min(max(need, 16 << 20), VMEM_LIMIT_CAP))


def _nbytes(shape, dtype):
    return int(np.prod(shape)) * jnp.dtype(dtype).itemsize


def _params(n_grid, vmem):
    return pltpu.CompilerParams(dimension_semantics=("arbitrary",) * n_grid, vmem_limit_bytes=vmem)


def _sigmoid(x):
    return 1.0 / (1.0 + jnp.exp(-x))


def _mod_kernel(c_ref, w_ref, b_ref, o_ref):
    c = c_ref[...]
    o_ref[...] = jnp.dot((c * _sigmoid(c)).astype(BF16), w_ref[...], preferred_element_type=F32) + b_ref[...]


def _mod_call(c_rows, w, b, l):
    m, d = c_rows.shape
    n = w.shape[2]
    tn = 1536
    return pl.pallas_call(
        _mod_kernel,
        grid=(n // tn,),
        in_specs=[pl.BlockSpec((m, d), lambda j: (0, 0)),
                  pl.BlockSpec((None, d, tn), lambda j: (l, 0, j)),
                  pl.BlockSpec((None, 1, tn), lambda j: (l, 0, j))],
        out_specs=pl.BlockSpec((m, tn), lambda j: (0, j)),
        out_shape=jax.ShapeDtypeStruct((m, n), F32),
        compiler_params=_params(1, _vmem_limit(_nbytes((d, tn), BF16))),
        name="mod_dense",
    )(c_rows, w, b)


def _modulate_kernel(x_ref, g_ref, sh_ref, sc_ref, o_ref):
    x = x_ref[0]
    y = x * lax.rsqrt(jnp.mean(x * x, axis=-1, keepdims=True) + EPS) * g_ref[...]
    o_ref[0] = (y * (1.0 + sc_ref[0]) + sh_ref[0]).astype(BF16)


def _modulate_call(x, g, shift, scale, tm):
    b_, t_, d = x.shape
    return pl.pallas_call(
        _modulate_kernel,
        grid=(b_, t_ // tm),
        in_specs=[pl.BlockSpec((1, tm, d), lambda b, i: (b, i, 0)),
                  pl.BlockSpec((1, d), lambda b, i: (0, 0)),
                  pl.BlockSpec((1, 1, d), lambda b, i: (b, 0, 0)),
                  pl.BlockSpec((1, 1, d), lambda b, i: (b, 0, 0))],
        out_specs=pl.BlockSpec((1, tm, d), lambda b, i: (b, i, 0)),
        out_shape=jax.ShapeDtypeStruct((b_, t_, d), BF16),
        compiler_params=_params(2, _vmem_limit(_nbytes((tm, d), F32) + _nbytes((tm, d), BF16),
                                               2 * _nbytes((tm, d), F32))),
        name="modulate",
    )(x, g, shift, scale)


def _proj_kernel(a_ref, w_ref, b_ref, *rest, epilogue):
    acc = jnp.dot(a_ref[0], w_ref[...], preferred_element_type=F32) + b_ref[...]
    epilogue(acc, *rest)


def _proj_call(a, wb, l, epilogue, extras, extra_specs, out_shape, out_spec, *, tm, tn, name):
    w, bias = wb
    b_, t_, k = a.shape
    n = w.shape[2]
    blocks = _nbytes((tm, k), BF16) + _nbytes((k, tn), BF16) + 2 * _nbytes((tm, tn), F32)
    return pl.pallas_call(
        functools.partial(_proj_kernel, epilogue=epilogue),
        grid=(b_, t_ // tm, n // tn),
        in_specs=[pl.BlockSpec((1, tm, k), lambda b, i, j: (b, i, 0)),
                  pl.BlockSpec((None, k, tn), lambda b, i, j: (l, 0, j)),
                  pl.BlockSpec((None, 1, tn), lambda b, i, j: (l, 0, j))] + list(extra_specs),
        out_specs=out_spec,
        out_shape=out_shape,
        compiler_params=_params(3, _vmem_limit(blocks, 3 * _nbytes((tm, tn), F32))),
        name=name,
    )(a, w, bias, *extras)


def _ep_plain(acc, o_ref):
    o_ref[0] = acc.astype(o_ref.dtype)


def _ep_sigmoid(acc, o_ref):
    o_ref[0] = _sigmoid(acc)


def _ep_rope(acc, tab_ref, o_ref):
    cos = tab_ref[0]
    sin = tab_ref[1]
    for h in range(acc.shape[1] // A_DK):
        x = acc[:, h * A_DK:(h + 1) * A_DK]
        o_ref[0, :, h * A_DK:(h + 1) * A_DK] = (x * cos + pltpu.roll(x, A_DK // 2, 1) * sin).astype(o_ref.dtype)


def _ep_outgate(acc, o_ref):
    half = acc.shape[1] // 2
    az = acc[:, half:]
    o_ref[0] = _sigmoid(acc[:, :half]) * (az * _sigmoid(az))


def _ep_headnorm(acc, gain_ref, o_ref):
    for h in range(acc.shape[1] // B_DH):
        x = acc[:, h * B_DH:(h + 1) * B_DH]
        y = x * lax.rsqrt(jnp.mean(x * x, axis=-1, keepdims=True) + EPS)
        o_ref[0, h] = (y * gain_ref[:, h * B_DH:(h + 1) * B_DH]).astype(o_ref.dtype)


def _ep_headmajor(acc, o_ref):
    for h in range(acc.shape[1] // B_DH):
        o_ref[0, h] = acc[:, h * B_DH:(h + 1) * B_DH].astype(o_ref.dtype)


def _ep_headmajor_silu(acc, o_ref):
    for h in range(acc.shape[1] // B_DH):
        x = acc[:, h * B_DH:(h + 1) * B_DH]
        o_ref[0, h] = (x * _sigmoid(x)).astype(o_ref.dtype)


def _running_max_rows(x, reverse):
    n = x.shape[0]
    rowi = lax.broadcasted_iota(jnp.int32, x.shape, 0)
    s = 1
    while s < n:
        if s < SUBLANES:
            if reverse:
                sh = jnp.where(rowi < n - s, pltpu.roll(x, n - s, 0), -jnp.inf)
            else:
                sh = jnp.where(rowi >= s, pltpu.roll(x, s, 0), -jnp.inf)
        else:
            pad = jnp.full((s, x.shape[1]), -jnp.inf, x.dtype)
            sh = jnp.concatenate([x[s:], pad], 0) if reverse else jnp.concatenate([pad, x[:n - s]], 0)
        x = jnp.maximum(x, sh)
        s *= 2
    return x


def _mlstm_kernel(qf_ref, kf_ref, vf_ref, gf_ref, qb_ref, kb_ref, vb_ref, gb_ref,
                  c0_ref, n0_ref, m0_ref, hf_ref, hb_ref, c_ref, n_ref, m_ref, *, chunk):
    L = chunk

    @pl.when(pl.program_id(1) == 0)
    def _():
        c_ref[...] = c0_ref[...]
        n_ref[...] = n0_ref[...]
        m_ref[...] = m0_ref[...]

    row = lax.broadcasted_iota(jnp.int32, (L, L), 0)
    col = lax.broadcasted_iota(jnp.int32, (L, L), 1)
    lane = lax.broadcasted_iota(jnp.int32, (1, LANES), 1)
    m_all = m_ref[0]
    m_next = m_all
    dirs = ((qf_ref, kf_ref, vf_ref, gf_ref, hf_ref, row >= col),
            (qb_ref, kb_ref, vb_ref, gb_ref, hb_ref, row <= col))
    for d, (q_ref, k_ref, v_ref, g_ref, h_ref, mask) in enumerate(dirs):
        gates = g_ref[0]
        ig = gates[:, :LANES]
        fg = gates[:, LANES:]
        logf = -(jnp.maximum(-fg, 0.0) + jnp.log1p(jnp.exp(-jnp.abs(fg))))
        bcum = jnp.dot(mask.astype(F32), logf, precision=lax.Precision.HIGHEST, preferred_element_type=F32)
        a = ig - bcum
        m_run = jnp.maximum(_running_max_rows(a, reverse=(d == 1)), m_all)
        w_int_all = jnp.exp(m_all - m_run)
        e_mq_all = jnp.exp(-(bcum + m_run))
        last = L - 1 if d == 0 else 0
        m_last = m_run[last:last + 1]
        w_prev_all = jnp.exp(m_all - m_last)
        w_tok_all = jnp.exp(a - m_last)
        in_dir = (lane >= d * A_HEADS) & (lane < (d + 1) * A_HEADS)
        m_next = jnp.where(in_dir, bcum[last:last + 1] + m_last, m_next)
        a_t = a.T
        for h in range(A_HEADS):
            ch = d * A_HEADS + h
            q = q_ref[0, :, h * A_DK:(h + 1) * A_DK]
            k = k_ref[0, :, h * A_DK:(h + 1) * A_DK]
            v = v_ref[0, :, h * A_DV:(h + 1) * A_DV]
            c_st = c_ref[0, ch]
            n_st = n_ref[0, ch]
            decay = jnp.where(mask, jnp.exp(a_t[ch:ch + 1, :] - m_run[:, ch:ch + 1]), 0.0)
            s = lax.dot_general(q, k, (((1,), (1,)), ((), ())), preferred_element_type=F32) * decay
            w_int = w_int_all[:, ch:ch + 1]
            den = (jnp.sum(s, axis=1, keepdims=True)
                   + w_int * jnp.sum(q.astype(F32) * n_st, axis=1, keepdims=True))
            r = 1.0 / jnp.maximum(jnp.abs(den), e_mq_all[:, ch:ch + 1])
            h_ref[0, :, h * A_DV:(h + 1) * A_DV] = (
                jnp.dot(s.astype(BF16), v.astype(BF16), preferred_element_type=F32) * r
                + jnp.dot(q, c_st.astype(BF16), preferred_element_type=F32) * (w_int * r))

            w_prev = w_prev_all[:, ch:ch + 1]
            w_tok = w_tok_all[:, ch:ch + 1]
            c_ref[0, ch] = w_prev * c_st + lax.dot_general(
                k, (w_tok * v).astype(BF16), (((0,), (0,)), ((), ())), preferred_element_type=F32)
            n_ref[0, ch] = w_prev * n_st + jnp.sum(w_tok * k.astype(F32), axis=0, keepdims=True)
    m_ref[0] = m_next


def _mlstm_call(qk, v, gates, state, *, chunk, name):
    b_, t_, _ = v.shape
    nc = t_ // chunk
    wq = A_HEADS * A_DK
    wv = A_HEADS * A_DV
    c0, n0, m0 = state
    fwd = lambda b, c: c
    bwd = lambda b, c: nc - 1 - c

    def spec(width, blk, which):
        return pl.BlockSpec((1, chunk, width), lambda b, c: (b, which(b, c), blk))

    st_specs = [pl.BlockSpec((1, 2 * A_HEADS, A_DK, A_DV), lambda b, c: (b, 0, 0, 0)),
                pl.BlockSpec((1, 2 * A_HEADS, 1, A_DK), lambda b, c: (b, 0, 0, 0)),
                pl.BlockSpec((1, 1, LANES), lambda b, c: (b, 0, 0))]
    blocks = (4 * _nbytes((chunk, wq), BF16) + 4 * _nbytes((chunk, wv), F32)
              + 2 * _nbytes((chunk, 2 * LANES), F32) + 2 * _nbytes((2 * A_HEADS, A_DK, A_DV), F32))
    return pl.pallas_call(
        functools.partial(_mlstm_kernel, chunk=chunk),
        grid=(b_, nc),
        in_specs=[spec(wq, 0, fwd), spec(wq, 1, fwd), spec(wv, 0, fwd), spec(2 * LANES, 0, fwd),
                  spec(wq, 0, bwd), spec(wq, 1, bwd), spec(wv, 0, bwd), spec(2 * LANES, 0, bwd)] + st_specs,
        out_specs=[spec(wv, 0, fwd), spec(wv, 0, bwd)] + st_specs,
        out_shape=[jax.ShapeDtypeStruct((b_, t_, wv), F32), jax.ShapeDtypeStruct((b_, t_, wv), F32),
                   jax.ShapeDtypeStruct(c0.shape, F32), jax.ShapeDtypeStruct(n0.shape, F32),
                   jax.ShapeDtypeStruct(m0.shape, F32)],
        compiler_params=_params(2, _vmem_limit(blocks, 24 * _nbytes((chunk, chunk), F32))),
        name=name,
    )(qk, qk, v, gates, qk, qk, v, gates, c0, n0, m0)


def _mlstm_out_kernel(hf_ref, hb_ref, og_ref, g_ref, o_ref):
    for h in range(A_HEADS):
        sl = slice(h * A_DV, (h + 1) * A_DV)
        x = hf_ref[0, :, sl] + hb_ref[0, :, sl]
        y = x * lax.rsqrt(jnp.mean(x * x, axis=-1, keepdims=True) + EPS) * g_ref[:, sl]
        o_ref[0, :, sl] = (y * og_ref[0, :, sl]).astype(BF16)


def _mlstm_out_call(hf, hb, og, g, l, tm):
    b_, t_, w = hf.shape
    spec = pl.BlockSpec((1, tm, w), lambda b, i: (b, i, 0))
    return pl.pallas_call(
        _mlstm_out_kernel,
        grid=(b_, t_ // tm),
        in_specs=[spec, spec, spec, pl.BlockSpec((None, 1, w), lambda b, i: (l, 0, 0))],
        out_specs=spec,
        out_shape=jax.ShapeDtypeStruct((b_, t_, w), BF16),
        compiler_params=_params(2, _vmem_limit(4 * _nbytes((tm, w), F32), 2 * _nbytes((tm, w), F32))),
        name="mlstm_out",
    )(hf, hb, og, g)


def _na_kernel(q_ref, k_ref, v_ref, kc_ref, vc_ref, z_ref, bias_ref, o_ref, *, rows, rows_per_step):
    nq = NA_QROWS * GRID_W
    nk = NA_KROWS * GRID_W
    r0 = pl.program_id(2) * rows_per_step
    kc = kc_ref[0, 0]
    vc = vc_ref[0, 0]

    def body(it, carry):
        r = r0 + it * NA_QROWS
        base = jnp.clip(r - NA_KH // 2, 0, rows - NA_KROWS)
        variant = jnp.where(r == 0, 0, jnp.where(r == 2, 1, jnp.where(r == rows - 4, 3, jnp.where(r == rows - 2, 4, 2))))
        qs = pl.multiple_of(it * nq, nq)
        ks = pl.multiple_of(base * GRID_W, GRID_W)
        q = q_ref[0, 0, pl.ds(qs, nq), :]
        k = k_ref[0, 0, pl.ds(ks, nk), :]
        v = v_ref[0, 0, pl.ds(ks, nk), :]
        s = lax.dot_general(q, k, (((1,), (1,)), ((), ())), preferred_element_type=F32) + bias_ref[variant]
        sc = lax.dot_general(q, kc, (((1,), (1,)), ((), ())), preferred_element_type=F32)
        m = jnp.maximum(jnp.max(s, axis=1, keepdims=True), jnp.max(sc, axis=1, keepdims=True))
        p = jnp.exp(s - m)
        pc = jnp.exp(sc - m)
        rl = 1.0 / (jnp.sum(p, axis=1, keepdims=True) + jnp.sum(pc, axis=1, keepdims=True))
        o = (jnp.dot(p.astype(BF16), v, preferred_element_type=F32)
             + jnp.dot(pc.astype(BF16), vc, preferred_element_type=F32))
        o_ref[0, 0, pl.ds(qs, nq), :] = (o * rl * z_ref[0, 0, pl.ds(qs, nq), :]).astype(BF16)
        return carry

    n_it = rows_per_step // NA_QROWS
    lax.fori_loop(0, n_it, body, 0, unroll=min(NA_UNROLL, n_it))


def _na_call(qk, v, qk_ctx, v_ctx, z, bias, l):
    b_, _, t_, dh = v.shape
    tc = v_ctx.shape[2]
    rows = t_ // GRID_W
    rows_per_step = min(rows, 64)
    tq = rows_per_step * GRID_W
    nvar, nq, nk = bias.shape[2:]
    blocks = (2 * _nbytes((tq, dh), BF16) + 2 * _nbytes((t_, dh), BF16) + 2 * _nbytes((tc, dh), BF16)
              + _nbytes((tq, dh), F32) + _nbytes((nvar, nq, nk), F32))
    return pl.pallas_call(
        functools.partial(_na_kernel, rows=rows, rows_per_step=rows_per_step),
        grid=(b_, B_HEADS, t_ // tq),
        in_specs=[pl.BlockSpec((1, 1, tq, dh), lambda b, h, i: (b, h, i, 0)),
                  pl.BlockSpec((1, 1, t_, dh), lambda b, h, i: (b, B_HEADS + h, 0, 0)),
                  pl.BlockSpec((1, 1, t_, dh), lambda b, h, i: (b, h, 0, 0)),
                  pl.BlockSpec((1, 1, tc, dh), lambda b, h, i: (b, B_HEADS + h, 0, 0)),
                  pl.BlockSpec((1, 1, tc, dh), lambda b, h, i: (b, h, 0, 0)),
                  pl.BlockSpec((1, 1, tq, dh), lambda b, h, i: (b, h, i, 0)),
                  pl.BlockSpec((None, None, nvar, nq, nk), lambda b, h, i: (l, h, 0, 0, 0))],
        out_specs=pl.BlockSpec((1, 1, tq, dh), lambda b, h, i: (b, h, i, 0)),
        out_shape=jax.ShapeDtypeStruct((b_, B_HEADS, t_, dh), BF16),
        compiler_params=_params(3, _vmem_limit(blocks, 8 * NA_UNROLL * _nbytes((nq, nk + tc), F32))),
        name="na_attention",
    )(qk, qk, v, qk_ctx, v_ctx, z, bias)


def _ctx_attn_kernel(q_ref, k_ref, v_ref, z_ref, o_ref):
    q = q_ref[0, 0]
    s = lax.dot_general(q, k_ref[0, 0], (((1,), (1,)), ((), ())), preferred_element_type=F32)
    m = jnp.max(s, axis=1, keepdims=True)
    p = jnp.exp(s - m)
    rl = 1.0 / jnp.sum(p, axis=1, keepdims=True)
    o = jnp.dot(p.astype(BF16), v_ref[0, 0], preferred_element_type=F32)
    o_ref[0, 0] = (o * rl * z_ref[0, 0]).astype(BF16)


def _ctx_attn_call(qk_ctx, v_ctx, z_ctx):
    b_, _, tc, dh = v_ctx.shape
    blk = (1, 1, tc, dh)
    return pl.pallas_call(
        _ctx_attn_kernel,
        grid=(b_, B_HEADS),
        in_specs=[pl.BlockSpec(blk, lambda b, h: (b, h, 0, 0)),
                  pl.BlockSpec(blk, lambda b, h: (b, B_HEADS + h, 0, 0)),
                  pl.BlockSpec(blk, lambda b, h: (b, h, 0, 0)),
                  pl.BlockSpec(blk, lambda b, h: (b, h, 0, 0))],
        out_specs=pl.BlockSpec(blk, lambda b, h: (b, h, 0, 0)),
        out_shape=jax.ShapeDtypeStruct((b_, B_HEADS, tc, dh), BF16),
        compiler_params=_params(2, _vmem_limit(5 * _nbytes((tc, dh), F32), 4 * _nbytes((tc, tc), F32))),
        name="ctx_attention",
    )(qk_ctx, qk_ctx, v_ctx, z_ctx)


def _merge1_kernel(ya_ref, yb_ref, wa_ref, wb_ref, sa_ref, sb_ref, o_ref, ybcat_ref):
    @pl.when(pl.program_id(2) == 0)
    def _():
        for h in range(B_HEADS):
            ybcat_ref[:, h * B_DH:(h + 1) * B_DH] = yb_ref[0, h]

    ua = jnp.dot(ya_ref[0], wa_ref[...], preferred_element_type=F32)
    ub = jnp.dot(ybcat_ref[...], wb_ref[...], preferred_element_type=F32)
    o_ref[0] = (sa_ref[0] * ua + sb_ref[0] * ub).astype(BF16)


def _merge1_call(ya, yb, wa, wb, sgab, l, *, tm, tn):
    b_, t_, d = ya.shape
    nj = d // tn
    blocks = (2 * _nbytes((tm, d), BF16) + 2 * _nbytes((d, tn), BF16) + 2 * _nbytes((tm, tn), F32)
              + _nbytes((tm, tn), BF16))
    return pl.pallas_call(
        _merge1_kernel,
        grid=(b_, t_ // tm, nj),
        in_specs=[pl.BlockSpec((1, tm, d), lambda b, i, j: (b, i, 0)),
                  pl.BlockSpec((1, B_HEADS, tm, B_DH), lambda b, i, j: (b, 0, i, 0)),
                  pl.BlockSpec((None, d, tn), lambda b, i, j: (l, 0, j)),
                  pl.BlockSpec((None, d, tn), lambda b, i, j: (l, 0, j)),
                  pl.BlockSpec((1, tm, tn), lambda b, i, j: (b, i, j)),
                  pl.BlockSpec((1, tm, tn), lambda b, i, j: (b, i, nj + j))],
        out_specs=pl.BlockSpec((1, tm, tn), lambda b, i, j: (b, i, j)),
        out_shape=jax.ShapeDtypeStruct((b_, t_, d), BF16),
        scratch_shapes=[pltpu.VMEM((tm, d), BF16)],
        compiler_params=_params(3, _vmem_limit(blocks, _nbytes((tm, d), BF16) + 3 * _nbytes((tm, tn), F32))),
        name="merge_branches",
    )(ya, yb, wa, wb, sgab, sgab)


def _merge2_kernel(u_ref, w_ref, b_ref, x_ref, g_ref, o_ref):
    y = jnp.dot(u_ref[0], w_ref[...], preferred_element_type=F32) + b_ref[...]
    o_ref[0] = x_ref[0] + g_ref[0] * y


def _merge2_call(u, wo, bo, x, gate, l, *, tm, tn):
    b_, t_, d = u.shape
    blocks = _nbytes((tm, d), BF16) + _nbytes((d, tn), BF16) + 2 * _nbytes((tm, tn), F32)
    return pl.pallas_call(
        _merge2_kernel,
        grid=(b_, t_ // tm, d // tn),
        in_specs=[pl.BlockSpec((1, tm, d), lambda b, i, j: (b, i, 0)),
                  pl.BlockSpec((None, d, tn), lambda b, i, j: (l, 0, j)),
                  pl.BlockSpec((None, 1, tn), lambda b, i, j: (l, 0, j)),
                  pl.BlockSpec((1, tm, tn), lambda b, i, j: (b, i, j)),
                  pl.BlockSpec((1, 1, tn), lambda b, i, j: (b, 0, j))],
        out_specs=pl.BlockSpec((1, tm, tn), lambda b, i, j: (b, i, j)),
        out_shape=jax.ShapeDtypeStruct((b_, t_, d), F32),
        compiler_params=_params(3, _vmem_limit(blocks, 2 * _nbytes((tm, tn), F32))),
        name="merge_out",
    )(u, wo, bo, x, gate)


def _in_offsets(d_model):
    a_width = A_HEADS * A_DV
    b_width = B_HEADS * B_DH
    sizes = (A_HEADS * A_DK, A_HEADS * A_DK, a_width, a_width, a_width, A_HEADS, A_HEADS, A_HEADS, A_HEADS,
             b_width, b_width, b_width, b_width, d_model, d_model)
    offs = np.concatenate([[0], np.cumsum(sizes)])
    return {n: (int(offs[i]), int(sizes[i])) for i, n in enumerate(IN_NAMES)}


def _rope_layout(cols):
    lead = cols.shape[:-1]
    return cols.reshape(lead + (A_HEADS, 2, 2, A_DK // 4)).swapaxes(-3, -2).reshape(lead + (A_HEADS * A_DK,))


def _prep_in_weights(w, b, na_q_g, na_k_g):
    offs = _in_offsets(w.shape[1])

    def seg(name):
        o, s = offs[name]
        return [w[..., o:o + s], b[..., o:o + s]]

    def cat(parts):
        return [jnp.concatenate([p[i] for p in parts], axis=-1) for i in range(2)]

    def interleave(p0, p1, width):
        def one(x0, x1):
            lead = x0.shape[:-1]
            x0 = x0.reshape(lead + (-1, 1, width))
            x1 = x1.reshape(lead + (-1, 1, width))
            return jnp.concatenate([x0, x1], axis=-2).reshape(lead + (-1,))
        return [one(p0[i], p1[i]) for i in range(2)]

    def pad_lanes(p, n):
        return [jnp.pad(x, [(0, 0)] * (x.ndim - 1) + [(0, n - x.shape[-1])]) for x in p]

    out = {}
    out['qk'] = cat([[_rope_layout(x) for x in seg('aq')], [_rope_layout(x) for x in seg('ak')]])
    out['av'] = seg('av')
    out['og'] = interleave(seg('ao'), seg('az'), 512)
    out['gates'] = cat([pad_lanes(cat([seg('ai_f'), seg('ai_b')]), LANES),
                        pad_lanes(cat([seg('af_f'), seg('af_b')]), LANES)])
    out['bqk'] = cat([seg('bq'), seg('bk')])
    out['bv'] = seg('bv')
    out['bz'] = seg('bz')
    out['gab'] = cat([seg('ga'), seg('gb')])
    out = {k: (wv.astype(BF16), bv[:, None, :].astype(F32)) for k, (wv, bv) in out.items()}
    scale = B_DH ** -0.5
    out['bqk_gain'] = jnp.concatenate([jnp.tile(na_q_g * scale, (1, B_HEADS)),
                                       jnp.tile(na_k_g, (1, B_HEADS))], axis=-1)[:, None, :].astype(F32)
    return out


def _rope_tables(n_tok, with_rope):
    nf = A_DK // 4
    kscale = A_DK ** -0.5
    if with_rope:
        t = jnp.arange(n_tok)
        rowp = (t // GRID_W).astype(F32)
        colp = (t % GRID_W).astype(F32)
        inv = ROPE_BASE ** (-jnp.arange(nf, dtype=F32) / nf)
        ar = rowp[:, None] * inv
        ac = colp[:, None] * inv
        cos = jnp.concatenate([jnp.cos(ar), jnp.cos(ac), jnp.cos(ar), jnp.cos(ac)], axis=1)
        sin = jnp.concatenate([-jnp.sin(ar), -jnp.sin(ac), jnp.sin(ar), jnp.sin(ac)], axis=1)
    else:
        cos = jnp.ones((n_tok, A_DK), F32)
        sin = jnp.zeros((n_tok, A_DK), F32)
    tq = jnp.stack([cos, sin])
    return jnp.stack([tq, tq * kscale])


def _na_bias_tables(rpb, rows):
    assert rows >= 2 * NA_KH, "neighbourhood attention blocks assume at least 16 grid rows"
    n_dr = 2 * NA_KH - 1
    n_dc = 2 * NA_KW - 1
    qc = np.arange(GRID_W)
    kc = np.arange(GRID_W)
    ws = np.clip(qc - NA_KW // 2, 0, GRID_W - NA_KW)
    cvalid = (kc[None, :] >= ws[:, None]) & (kc[None, :] < ws[:, None] + NA_KW)
    dc = np.clip(kc[None, :] - qc[:, None] + NA_KW - 1, 0, n_dc - 1)
    csel = np.eye(n_dc, dtype=np.float32)[dc.reshape(-1)]
    rsel, valid = [], []
    for r in (0, 2, 4, rows - 4, rows - 2):
        base = int(np.clip(r - NA_KH // 2, 0, rows - NA_KROWS))
        qr = r + np.arange(NA_QROWS)
        rs = np.clip(qr - NA_KH // 2, 0, rows - NA_KH)
        kr = base + np.arange(NA_KROWS)
        rvalid = (kr[None, :] >= rs[:, None]) & (kr[None, :] < rs[:, None] + NA_KH)
        dr = np.clip(kr[None, :] - qr[:, None] + NA_KH - 1, 0, n_dr - 1)
        rsel.append(np.eye(n_dr, dtype=np.float32)[dr.reshape(-1)])
        valid.append((rvalid[:, None, :, None] & cvalid[None, :, None, :]).reshape(NA_QROWS * GRID_W, NA_KROWS * GRID_W))
    rsel = jnp.asarray(np.stack(rsel))
    valid = jnp.asarray(np.stack(valid))
    hi = lax.Precision.HIGHEST
    t = jnp.einsum('xpu,lhuv->lhxpv', rsel, rpb.astype(F32), precision=hi)
    t = jnp.einsum('lhxpv,cv->lhxpc', t, jnp.asarray(csel), precision=hi)
    lead = t.shape[:3]
    t = t.reshape(lead + (NA_QROWS, NA_KROWS, GRID_W, GRID_W)).swapaxes(-3, -2)
    t = t.reshape(lead + (NA_QROWS * GRID_W, NA_KROWS * GRID_W))
    return jnp.where(valid, t, NEG_BIG)


def _project(h, wts, l, rope_tab, *, tm):
    b_, t_, d = h.shape
    tn = 1024
    res = {}
    res['qk'] = _proj_call(
        h, wts['qk'], l, _ep_rope, [rope_tab],
        [pl.BlockSpec((None, 2, tm, A_DK), lambda b, i, j: (j, 0, i, 0))],
        jax.ShapeDtypeStruct((b_, t_, 2 * A_HEADS * A_DK), BF16),
        pl.BlockSpec((1, tm, tn), lambda b, i, j: (b, i, j)), tm=tm, tn=tn, name="proj_qk_rope")
    res['av'] = _proj_call(
        h, wts['av'], l, _ep_plain, [], [],
        jax.ShapeDtypeStruct((b_, t_, A_HEADS * A_DV), F32),
        pl.BlockSpec((1, tm, tn), lambda b, i, j: (b, i, j)), tm=tm, tn=tn, name="proj_av")
    res['og'] = _proj_call(
        h, wts['og'], l, _ep_outgate, [], [],
        jax.ShapeDtypeStruct((b_, t_, A_HEADS * A_DV), F32),
        pl.BlockSpec((1, tm, tn // 2), lambda b, i, j: (b, i, j)), tm=tm, tn=tn, name="proj_outgate")
    res['gates'] = _proj_call(
        h, wts['gates'], l, _ep_plain, [], [],
        jax.ShapeDtypeStruct((b_, t_, 2 * LANES), F32),
        pl.BlockSpec((1, tm, 2 * LANES), lambda b, i, j: (b, i, j)), tm=tm, tn=2 * LANES, name="proj_gates")
    hpt = tn // B_DH
    hm_spec = pl.BlockSpec((1, hpt, tm, B_DH), lambda b, i, j: (b, j, i, 0))
    res['bqk'] = _proj_call(
        h, wts['bqk'], l, _ep_headnorm, [wts['bqk_gain']],
        [pl.BlockSpec((None, 1, tn), lambda b, i, j: (l, 0, j))],
        jax.ShapeDtypeStruct((b_, 2 * B_HEADS, t_, B_DH), BF16), hm_spec, tm=tm, tn=tn, name="proj_bqk_norm")
    res['bv'] = _proj_call(
        h, wts['bv'], l, _ep_headmajor, [], [],
        jax.ShapeDtypeStruct((b_, B_HEADS, t_, B_DH), BF16), hm_spec, tm=tm, tn=tn, name="proj_bv")
    res['bz'] = _proj_call(
        h, wts['bz'], l, _ep_headmajor_silu, [], [],
        jax.ShapeDtypeStruct((b_, B_HEADS, t_, B_DH), F32), hm_spec, tm=tm, tn=tn, name="proj_bz_silu")
    res['gab'] = _proj_call(
        h, wts['gab'], l, _ep_sigmoid, [], [],
        jax.ShapeDtypeStruct((b_, t_, 2 * d), F32),
        pl.BlockSpec((1, tm, tn), lambda b, i, j: (b, i, j)), tm=tm, tn=tn, name="proj_gab_sigmoid")
    return res


def kernel(x, c, ctx, c_ctx, w_mod, b_mod, norm_g, w_in, b_in, a_norm_g, w_br_a, w_br_b, na_q_g, na_k_g, na_rpb, w_out, b_out):
    bsz, n_tok, d = x.shape
    n_ctx = ctx.shape[1]
    depth = w_mod.shape[0]
    rows = n_tok // GRID_W
    assert n_tok % MLSTM_CHUNK == 0 and n_ctx % MLSTM_CHUNK == 0 and n_tok % (NA_QROWS * GRID_W) == 0
    mod_rows = 16
    assert bsz + 1 <= mod_rows
    tm_lat = min(1024, n_tok)
    tm_ctx = min(1024, n_ctx)
    te_lat = min(512, n_tok)
    te_ctx = min(512, n_ctx)

    rope_lat = _rope_tables(n_tok, True)
    rope_ctx = _rope_tables(n_ctx, False)
    wts = _prep_in_weights(w_in, b_in, na_q_g, na_k_g)
    bias = _na_bias_tables(na_rpb, rows)
    w_mod_b = w_mod.astype(BF16)
    b_mod_r = b_mod[:, None, :]
    wa = w_br_a.astype(BF16)
    wb = w_br_b.astype(BF16)
    wo = w_out.astype(BF16)
    bo = b_out[:, None, :]
    a_g = a_norm_g[:, None, :]
    c_rows = jnp.zeros((mod_rows, d), F32).at[:bsz].set(c).at[bsz].set(c_ctx)
    zero_state = (jnp.zeros((bsz, 2 * A_HEADS, A_DK, A_DV), F32),
                  jnp.zeros((bsz, 2 * A_HEADS, 1, A_DK), F32),
                  jnp.zeros((bsz, 1, LANES), F32))

    def merge(ya, yb, sgab, x_in, gate, l, tm):
        u = _merge1_call(ya, yb, wa, wb, sgab, l, tm=tm, tn=512)
        return _merge2_call(u, wo, bo, x_in, gate, l, tm=tm, tn=1024)

    for l in range(depth):
        need_ctx = l < depth - 1
        mod = _mod_call(c_rows, w_mod_b, b_mod_r, l)
        shift, scale, gate = (mod[:bsz, k * d:(k + 1) * d].reshape(bsz, 1, d) for k in range(3))
        shift_c, scale_c, gate_c = (jnp.broadcast_to(mod[bsz, k * d:(k + 1) * d].reshape(1, 1, d), (bsz, 1, d))
                                    for k in range(3))
        g_row = norm_g[l].reshape(1, d)
        h_lat = _modulate_call(x, g_row, shift, scale, te_lat)
        h_ctx = _modulate_call(ctx, g_row, shift_c, scale_c, te_ctx)
        pl_ = _project(h_lat, wts, l, rope_lat, tm=tm_lat)
        pc_ = _project(h_ctx, wts, l, rope_ctx, tm=tm_ctx)

        hcf, hcb, c1, n1, m1 = _mlstm_call(pc_['qk'], pc_['av'], pc_['gates'], zero_state,
                                           chunk=MLSTM_CHUNK, name="mlstm_ctx")
        hlf, hlb, _, _, _ = _mlstm_call(pl_['qk'], pl_['av'], pl_['gates'], (c1, n1, m1),
                                        chunk=MLSTM_CHUNK, name="mlstm_lat")
        ya_l = _mlstm_out_call(hlf, hlb, pl_['og'], a_g, l, te_lat)

        yb_l = _na_call(pl_['bqk'], pl_['bv'], pc_['bqk'], pc_['bv'], pl_['bz'], bias, l)

        x = merge(ya_l, yb_l, pl_['gab'], x, gate, l, tm_lat)
        if need_ctx:
            ya_c = _mlstm_out_call(hcf, hcb, pc_['og'], a_g, l, te_ctx)
            yb_c = _ctx_attn_call(pc_['bqk'], pc_['bv'], pc_['bz'])
            ctx = merge(ya_c, yb_c, pc_['gab'], ctx, gate_c, l, tm_ctx)
    return x
```

```python
import functools

import numpy as np
import jax
import jax.numpy as jnp
from jax import lax
from jax.experimental import pallas as pl
from jax.experimental.pallas import tpu as pltpu

F32 = jnp.float32
BF16 = jnp.bfloat16

GRID_W = 64
A_HEADS = 8
A_DK = 128
A_DV = 256
B_HEADS = 16
B_DH = 128
NA_KH = 8
NA_KW = 16
ROPE_BASE = 10000.0
EPS = 1e-6
IN_NAMES = ('aq', 'ak', 'av', 'ao', 'az', 'ai_f', 'af_f', 'ai_b', 'af_b', 'bq', 'bk', 'bv', 'bz', 'ga', 'gb')

LANES = 128
SUBLANES = 8
VMEM_LIMIT_CAP = 56 * 1024 * 1024

MLSTM_CHUNK = 256
NA_QROWS = 4
NA_KROWS = NA_KH + NA_QROWS - 1
NA_UNROLL = 2
NEG_BIG = -1e30


def _vmem_limit(block_bytes, temp_bytes=0):
    need = 2 * block_bytes + temp_bytes + (4 << 20)
    return int(min(max(need, 16 << 20), VMEM_LIMIT_CAP))


def _nbytes(shape, dtype):
    return int(np.prod(shape)) * jnp.dtype(dtype).itemsize


def _params(n_grid, vmem):
    return pltpu.CompilerParams(dimension_semantics=("arbitrary",) * n_grid, vmem_limit_bytes=vmem)


def _sigmoid(x):
    return 1.0 / (1.0 + jnp.exp(-x))


def _mod_kernel(c_ref, w_ref, b_ref, o_ref):
    c = c_ref[...]
    o_ref[...] = jnp.dot((c * _sigmoid(c)).astype(BF16), w_ref[...], preferred_element_type=F32) + b_ref[...]


def _mod_call(c_rows, w, b, l):
    m, d = c_rows.shape
    n = w.shape[2]
    tn = 1536
    return pl.pallas_call(
        _mod_kernel,
        grid=(n // tn,),
        in_specs=[pl.BlockSpec((m, d), lambda j: (0, 0)),
                  pl.BlockSpec((None, d, tn), lambda j: (l, 0, j)),
                  pl.BlockSpec((None, 1, tn), lambda j: (l, 0, j))],
        out_specs=pl.BlockSpec((m, tn), lambda j: (0, j)),
        out_shape=jax.ShapeDtypeStruct((m, n), F32),
        compiler_params=_params(1, _vmem_limit(_nbytes((d, tn), BF16))),
        name="mod_dense",
    )(c_rows, w, b)


def _modulate_kernel(x_ref, g_ref, sh_ref, sc_ref, o_ref):
    x = x_ref[0]
    y = x * lax.rsqrt(jnp.mean(x * x, axis=-1, keepdims=True) + EPS) * g_ref[...]
    o_ref[0] = (y * (1.0 + sc_ref[0]) + sh_ref[0]).astype(BF16)


def _modulate_call(x, g, shift, scale, tm):
    b_, t_, d = x.shape
    return pl.pallas_call(
        _modulate_kernel,
        grid=(b_, t_ // tm),
        in_specs=[pl.BlockSpec((1, tm, d), lambda b, i: (b, i, 0)),
                  pl.BlockSpec((1, d), lambda b, i: (0, 0)),
                  pl.BlockSpec((1, 1, d), lambda b, i: (b, 0, 0)),
                  pl.BlockSpec((1, 1, d), lambda b, i: (b, 0, 0))],
        out_specs=pl.BlockSpec((1, tm, d), lambda b, i: (b, i, 0)),
        out_shape=jax.ShapeDtypeStruct((b_, t_, d), BF16),
        compiler_params=_params(2, _vmem_limit(_nbytes((tm, d), F32) + _nbytes((tm, d), BF16),
                                               2 * _nbytes((tm, d), F32))),
        name="modulate",
    )(x, g, shift, scale)


def _proj_kernel(a_ref, w_ref, b_ref, *rest, epilogue):
    acc = jnp.dot(a_ref[0], w_ref[...], preferred_element_type=F32) + b_ref[...]
    epilogue(acc, *rest)


def _proj_pair_kernel(a_ref, w0_ref, b0_ref, w1_ref, b1_ref, *rest, epilogue):
    a = a_ref[0]
    acc0 = jnp.dot(a, w0_ref[...], preferred_element_type=F32) + b0_ref[...]
    acc1 = jnp.dot(a, w1_ref[...], preferred_element_type=F32) + b1_ref[...]
    epilogue(acc0, acc1, *rest)


def _proj_call(a, wb, l, cols, epilogue, extras, extra_specs, out_shape, out_spec, *, tm, tn, name):
    w, bias = wb
    b_, t_, k = a.shape
    ranges = cols if isinstance(cols[0], tuple) else (cols,)
    n = ranges[0][1]
    w_specs, w_args = [], []
    for c0, cn in ranges:
        assert cn == n and c0 % tn == 0 and n % tn == 0
        jb = c0 // tn
        w_specs += [pl.BlockSpec((None, k, tn), lambda b, i, j, jb=jb: (l, 0, jb + j)),
                    pl.BlockSpec((None, 1, tn), lambda b, i, j, jb=jb: (l, 0, jb + j))]
        w_args += [w, bias]
    blocks = _nbytes((tm, k), BF16) + len(ranges) * _nbytes((k, tn), BF16) + 2 * _nbytes((tm, tn), F32)
    body = _proj_kernel if len(ranges) == 1 else _proj_pair_kernel
    return pl.pallas_call(
        functools.partial(body, epilogue=epilogue),
        grid=(b_, t_ // tm, n // tn),
        in_specs=[pl.BlockSpec((1, tm, k), lambda b, i, j: (b, i, 0))] + w_specs + list(extra_specs),
        out_specs=out_spec,
        out_shape=out_shape,
        compiler_params=_params(3, _vmem_limit(blocks, (2 + len(ranges)) * _nbytes((tm, tn), F32))),
        name=name,
    )(a, *w_args, *extras)


def _ep_plain(acc, o_ref):
    o_ref[0] = acc.astype(o_ref.dtype)


def _ep_sigmoid(acc, o_ref):
    o_ref[0] = _sigmoid(acc)


def _ep_rope(acc, tab_ref, o_ref):
    cos = tab_ref[0]
    sin = tab_ref[1]
    quarter = A_DK // 4
    lane = lax.broadcasted_iota(jnp.int32, cos.shape, 1)
    first_half = (lane % (2 * quarter)) < quarter
    for h in range(acc.shape[1] // A_DK):
        x = acc[:, h * A_DK:(h + 1) * A_DK]
        partner = jnp.where(first_half, pltpu.roll(x, A_DK - quarter, 1), pltpu.roll(x, quarter, 1))
        o_ref[0, :, h * A_DK:(h + 1) * A_DK] = (x * cos + partner * sin).astype(o_ref.dtype)


def _ep_outgate(acc_o, acc_z, o_ref):
    o_ref[0] = _sigmoid(acc_o) * (acc_z * _sigmoid(acc_z))


def _ep_headnorm(acc, gain_ref, o_ref):
    for h in range(acc.shape[1] // B_DH):
        x = acc[:, h * B_DH:(h + 1) * B_DH]
        y = x * lax.rsqrt(jnp.mean(x * x, axis=-1, keepdims=True) + EPS)
        o_ref[0, h] = (y * gain_ref[:, h * B_DH:(h + 1) * B_DH]).astype(o_ref.dtype)


def _ep_headmajor(acc, o_ref):
    for h in range(acc.shape[1] // B_DH):
        o_ref[0, h] = acc[:, h * B_DH:(h + 1) * B_DH].astype(o_ref.dtype)


def _ep_headmajor_silu(acc, o_ref):
    for h in range(acc.shape[1] // B_DH):
        x = acc[:, h * B_DH:(h + 1) * B_DH]
        o_ref[0, h] = (x * _sigmoid(x)).astype(o_ref.dtype)


def _running_max_rows(x, reverse):
    n = x.shape[0]
    rowi = lax.broadcasted_iota(jnp.int32, x.shape, 0)
    s = 1
    while s < n:
        if s < SUBLANES:
            if reverse:
                sh = jnp.where(rowi < n - s, pltpu.roll(x, n - s, 0), -jnp.inf)
            else:
                sh = jnp.where(rowi >= s, pltpu.roll(x, s, 0), -jnp.inf)
        else:
            pad = jnp.full((s, x.shape[1]), -jnp.inf, x.dtype)
            sh = jnp.concatenate([x[s:], pad], 0) if reverse else jnp.concatenate([pad, x[:n - s]], 0)
        x = jnp.maximum(x, sh)
        s *= 2
    return x


def _mlstm_kernel(*refs, chunk, dirs, fuse_out):
    L = chunk
    nd = len(dirs)
    in_refs = [refs[4 * i:4 * i + 4] for i in range(nd)]
    c0_ref, n0_ref, m0_ref = refs[4 * nd:4 * nd + 3]
    pos = 4 * nd + 3
    if fuse_out:
        hprev_ref, og_ref, gain_ref = refs[pos:pos + 3]
        pos += 3
    out_refs = refs[pos:pos + nd]
    c_ref, n_ref, m_ref = refs[pos + nd:pos + nd + 3]

    @pl.when(pl.program_id(1) == 0)
    def _():
        c_ref[...] = c0_ref[...]
        n_ref[...] = n0_ref[...]
        m_ref[...] = m0_ref[...]

    row = lax.broadcasted_iota(jnp.int32, (L, L), 0)
    col = lax.broadcasted_iota(jnp.int32, (L, L), 1)
    lane = lax.broadcasted_iota(jnp.int32, (1, LANES), 1)
    m_all = m_ref[0]
    m_next = m_all
    for d, (q_ref, k_ref, v_ref, g_ref), h_ref in zip(dirs, in_refs, out_refs):
        mask = (row >= col) if d == 0 else (row <= col)
        gates = g_ref[0]
        ig = gates[:, :LANES]
        fg = gates[:, LANES:]
        logf = -(jnp.maximum(-fg, 0.0) + jnp.log1p(jnp.exp(-jnp.abs(fg))))
        bcum = jnp.dot(mask.astype(F32), logf, precision=lax.Precision.HIGHEST, preferred_element_type=F32)
        a = ig - bcum
        m_run = jnp.maximum(_running_max_rows(a, reverse=(d == 1)), m_all)
        w_int_all = jnp.exp(m_all - m_run)
        e_mq_all = jnp.exp(-(bcum + m_run))
        last = L - 1 if d == 0 else 0
        m_last = m_run[last:last + 1]
        w_prev_all = jnp.exp(m_all - m_last)
        w_tok_all = jnp.exp(a - m_last)
        in_dir = (lane >= d * A_HEADS) & (lane < (d + 1) * A_HEADS)
        m_next = jnp.where(in_dir, bcum[last:last + 1] + m_last, m_next)
        a_t = a.T
        for h in range(A_HEADS):
            ch = d * A_HEADS + h
            q = q_ref[0, :, h * A_DK:(h + 1) * A_DK]
            k = k_ref[0, :, h * A_DK:(h + 1) * A_DK]
            v = v_ref[0, :, h * A_DV:(h + 1) * A_DV]
            c_st = c_ref[0, ch]
            n_st = n_ref[0, ch]
            decay = jnp.where(mask, jnp.exp(a_t[ch:ch + 1, :] - m_run[:, ch:ch + 1]), 0.0)
            s = lax.dot_general(q, k, (((1,), (1,)), ((), ())), preferred_element_type=F32) * decay
            w_int = w_int_all[:, ch:ch + 1]
            den = (jnp.sum(s, axis=1, keepdims=True)
                   + w_int * jnp.sum(q.astype(F32) * n_st, axis=1, keepdims=True))
            r = 1.0 / jnp.maximum(jnp.abs(den), e_mq_all[:, ch:ch + 1])
            hv = (jnp.dot(s.astype(BF16), v.astype(BF16), preferred_element_type=F32) * r
                  + jnp.dot(q, c_st.astype(BF16), preferred_element_type=F32) * (w_int * r))
            sl = slice(h * A_DV, (h + 1) * A_DV)
            if fuse_out:
                x = hprev_ref[0, :, sl] + hv
                y = x * lax.rsqrt(jnp.mean(x * x, axis=-1, keepdims=True) + EPS) * gain_ref[:, sl]
                h_ref[0, :, sl] = (y * og_ref[0, :, sl]).astype(h_ref.dtype)
            else:
                h_ref[0, :, sl] = hv

            w_prev = w_prev_all[:, ch:ch + 1]
            w_tok = w_tok_all[:, ch:ch + 1]
            c_ref[0, ch] = w_prev * c_st + lax.dot_general(
                k, (w_tok * v).astype(BF16), (((0,), (0,)), ((), ())), preferred_element_type=F32)
            n_ref[0, ch] = w_prev * n_st + jnp.sum(w_tok * k.astype(F32), axis=0, keepdims=True)
    m_ref[0] = m_next


def _mlstm_call(qk, v, gates, state, *, chunk, dirs, name, fuse=None):
    b_, t_, _ = v.shape
    nc = t_ // chunk
    wq = A_HEADS * A_DK
    wv = A_HEADS * A_DV
    c0, n0, m0 = state

    def spec(width, blk, d):
        which = (lambda c: c) if d == 0 else (lambda c: nc - 1 - c)
        return pl.BlockSpec((1, chunk, width), lambda b, c: (b, which(c), blk))

    st_specs = [pl.BlockSpec((1, 2 * A_HEADS, A_DK, A_DV), lambda b, c: (b, 0, 0, 0)),
                pl.BlockSpec((1, 2 * A_HEADS, 1, A_DK), lambda b, c: (b, 0, 0, 0)),
                pl.BlockSpec((1, 1, LANES), lambda b, c: (b, 0, 0))]
    in_specs, args = [], []
    for d in dirs:
        in_specs += [spec(wq, 0, d), spec(wq, 1, d), spec(wv, 0, d), spec(2 * LANES, 0, d)]
        args += [qk, qk, v, gates]
    in_specs += st_specs
    args += [c0, n0, m0]
    out_dtype = F32
    nblk = len(dirs)
    if fuse is not None:
        assert len(dirs) == 1
        h_prev, og, gain, l = fuse
        in_specs += [spec(wv, 0, dirs[0]), spec(wv, 0, dirs[0]),
                     pl.BlockSpec((None, 1, wv), lambda b, c: (l, 0, 0))]
        args += [h_prev, og, gain]
        out_dtype = BF16
        nblk += 2
    blocks = (len(dirs) * (2 * _nbytes((chunk, wq), BF16) + _nbytes((chunk, 2 * LANES), F32))
              + (len(dirs) + nblk) * _nbytes((chunk, wv), F32) + 2 * _nbytes((2 * A_HEADS, A_DK, A_DV), F32))
    return pl.pallas_call(
        functools.partial(_mlstm_kernel, chunk=chunk, dirs=tuple(dirs), fuse_out=fuse is not None),
        grid=(b_, nc),
        in_specs=in_specs,
        out_specs=[spec(wv, 0, d) for d in dirs] + st_specs,
        out_shape=[jax.ShapeDtypeStruct((b_, t_, wv), out_dtype) for _ in dirs]
                  + [jax.ShapeDtypeStruct(c0.shape, F32), jax.ShapeDtypeStruct(n0.shape, F32),
                     jax.ShapeDtypeStruct(m0.shape, F32)],
        compiler_params=_params(2, _vmem_limit(blocks, 24 * _nbytes((chunk, chunk), F32))),
        name=name,
    )(*args)


def _mlstm_out_kernel(hf_ref, hb_ref, og_ref, g_ref, o_ref):
    for h in range(A_HEADS):
        sl = slice(h * A_DV, (h + 1) * A_DV)
        x = hf_ref[0, :, sl] + hb_ref[0, :, sl]
        y = x * lax.rsqrt(jnp.mean(x * x, axis=-1, keepdims=True) + EPS) * g_ref[:, sl]
        o_ref[0, :, sl] = (y * og_ref[0, :, sl]).astype(BF16)


def _mlstm_out_call(hf, hb, og, g, l, tm):
    b_, t_, w = hf.shape
    spec = pl.BlockSpec((1, tm, w), lambda b, i: (b, i, 0))
    return pl.pallas_call(
        _mlstm_out_kernel,
        grid=(b_, t_ // tm),
        in_specs=[spec, spec, spec, pl.BlockSpec((None, 1, w), lambda b, i: (l, 0, 0))],
        out_specs=spec,
        out_shape=jax.ShapeDtypeStruct((b_, t_, w), BF16),
        compiler_params=_params(2, _vmem_limit(4 * _nbytes((tm, w), F32), 2 * _nbytes((tm, w), F32))),
        name="mlstm_out",
    )(hf, hb, og, g)


def _na_kernel(q_ref, k_ref, v_ref, kc_ref, vc_ref, z_ref, bias_ref, o_ref, *, rows, rows_per_step):
    nq = NA_QROWS * GRID_W
    nk = NA_KROWS * GRID_W
    r0 = pl.program_id(2) * rows_per_step
    kc = kc_ref[0, 0]
    vc = vc_ref[0, 0]

    def body(it, carry):
        r = r0 + it * NA_QROWS
        base = jnp.clip(r - NA_KH // 2, 0, rows - NA_KROWS)
        variant = jnp.where(r == 0, 0, jnp.where(r == rows - NA_QROWS, 2, 1))
        qs = pl.multiple_of(it * nq, nq)
        ks = pl.multiple_of(base * GRID_W, GRID_W)
        q = q_ref[0, 0, pl.ds(qs, nq), :]
        k = k_ref[0, 0, pl.ds(ks, nk), :]
        v = v_ref[0, 0, pl.ds(ks, nk), :]
        s = lax.dot_general(q, k, (((1,), (1,)), ((), ())), preferred_element_type=F32) + bias_ref[variant]
        sc = lax.dot_general(q, kc, (((1,), (1,)), ((), ())), preferred_element_type=F32)
        m = jnp.maximum(jnp.max(s, axis=1, keepdims=True), jnp.max(sc, axis=1, keepdims=True))
        p = jnp.exp(s - m)
        pc = jnp.exp(sc - m)
        rl = 1.0 / (jnp.sum(p, axis=1, keepdims=True) + jnp.sum(pc, axis=1, keepdims=True))
        o = (jnp.dot(p.astype(BF16), v, preferred_element_type=F32)
             + jnp.dot(pc.astype(BF16), vc, preferred_element_type=F32))
        o_ref[0, 0, pl.ds(qs, nq), :] = (o * rl * z_ref[0, 0, pl.ds(qs, nq), :]).astype(BF16)
        return carry

    n_it = rows_per_step // NA_QROWS
    lax.fori_loop(0, n_it, body, 0, unroll=min(NA_UNROLL, n_it))


def _na_call(qk, v, qk_ctx, v_ctx, z, bias, l):
    b_, _, t_, dh = v.shape
    tc = v_ctx.shape[2]
    rows = t_ // GRID_W
    rows_per_step = min(rows, 64)
    tq = rows_per_step * GRID_W
    nvar, nq, nk = bias.shape[2:]
    blocks = (2 * _nbytes((tq, dh), BF16) + 2 * _nbytes((t_, dh), BF16) + 2 * _nbytes((tc, dh), BF16)
              + _nbytes((tq, dh), F32) + _nbytes((nvar, nq, nk), F32))
    return pl.pallas_call(
        functools.partial(_na_kernel, rows=rows, rows_per_step=rows_per_step),
        grid=(b_, B_HEADS, t_ // tq),
        in_specs=[pl.BlockSpec((1, 1, tq, dh), lambda b, h, i: (b, h, i, 0)),
                  pl.BlockSpec((1, 1, t_, dh), lambda b, h, i: (b, B_HEADS + h, 0, 0)),
                  pl.BlockSpec((1, 1, t_, dh), lambda b, h, i: (b, h, 0, 0)),
                  pl.BlockSpec((1, 1, tc, dh), lambda b, h, i: (b, B_HEADS + h, 0, 0)),
                  pl.BlockSpec((1, 1, tc, dh), lambda b, h, i: (b, h, 0, 0)),
                  pl.BlockSpec((1, 1, tq, dh), lambda b, h, i: (b, h, i, 0)),
                  pl.BlockSpec((None, None, nvar, nq, nk), lambda b, h, i: (l, h, 0, 0, 0))],
        out_specs=pl.BlockSpec((1, 1, tq, dh), lambda b, h, i: (b, h, i, 0)),
        out_shape=jax.ShapeDtypeStruct((b_, B_HEADS, t_, dh), BF16),
        compiler_params=_params(3, _vmem_limit(blocks, 8 * NA_UNROLL * _nbytes((nq, nk + tc), F32))),
        name="na_attention",
    )(qk, qk, v, qk_ctx, v_ctx, z, bias)


def _ctx_attn_kernel(q_ref, k_ref, v_ref, z_ref, o_ref):
    q = q_ref[0, 0]
    s = lax.dot_general(q, k_ref[0, 0], (((1,), (1,)), ((), ())), preferred_element_type=F32)
    m = jnp.max(s, axis=1, keepdims=True)
    p = jnp.exp(s - m)
    rl = 1.0 / jnp.sum(p, axis=1, keepdims=True)
    o = jnp.dot(p.astype(BF16), v_ref[0, 0], preferred_element_type=F32)
    o_ref[0, 0] = (o * rl * z_ref[0, 0]).astype(BF16)


def _ctx_attn_call(qk_ctx, v_ctx, z_ctx):
    b_, _, tc, dh = v_ctx.shape
    blk = (1, 1, tc, dh)
    return pl.pallas_call(
        _ctx_attn_kernel,
        grid=(b_, B_HEADS),
        in_specs=[pl.BlockSpec(blk, lambda b, h: (b, h, 0, 0)),
                  pl.BlockSpec(blk, lambda b, h: (b, B_HEADS + h, 0, 0)),
                  pl.BlockSpec(blk, lambda b, h: (b, h, 0, 0)),
                  pl.BlockSpec(blk, lambda b, h: (b, h, 0, 0))],
        out_specs=pl.BlockSpec(blk, lambda b, h: (b, h, 0, 0)),
        out_shape=jax.ShapeDtypeStruct((b_, B_HEADS, tc, dh), BF16),
        compiler_params=_params(2, _vmem_limit(5 * _nbytes((tc, dh), F32), 4 * _nbytes((tc, tc), F32))),
        name="ctx_attention",
    )(qk_ctx, qk_ctx, v_ctx, z_ctx)


def _merge1_kernel(ya_ref, yb_ref, wa_ref, wb_ref, sa_ref, sb_ref, o_ref, ybcat_ref):
    @pl.when(pl.program_id(2) == 0)
    def _():
        for h in range(B_HEADS):
            ybcat_ref[:, h * B_DH:(h + 1) * B_DH] = yb_ref[0, h]

    ua = jnp.dot(ya_ref[0], wa_ref[...], preferred_element_type=F32)
    ub = jnp.dot(ybcat_ref[...], wb_ref[...], preferred_element_type=F32)
    o_ref[0] = (sa_ref[0] * ua + sb_ref[0] * ub).astype(BF16)


def _merge1_call(ya, yb, wa, wb, sgab, l, *, tm, tn):
    b_, t_, d = ya.shape
    nj = d // tn
    blocks = (2 * _nbytes((tm, d), BF16) + 2 * _nbytes((d, tn), BF16) + 2 * _nbytes((tm, tn), F32)
              + _nbytes((tm, tn), BF16))
    return pl.pallas_call(
        _merge1_kernel,
        grid=(b_, t_ // tm, nj),
        in_specs=[pl.BlockSpec((1, tm, d), lambda b, i, j: (b, i, 0)),
                  pl.BlockSpec((1, B_HEADS, tm, B_DH), lambda b, i, j: (b, 0, i, 0)),
                  pl.BlockSpec((None, d, tn), lambda b, i, j: (l, 0, j)),
                  pl.BlockSpec((None, d, tn), lambda b, i, j: (l, 0, j)),
                  pl.BlockSpec((1, tm, tn), lambda b, i, j: (b, i, j)),
                  pl.BlockSpec((1, tm, tn), lambda b, i, j: (b, i, nj + j))],
        out_specs=pl.BlockSpec((1, tm, tn), lambda b, i, j: (b, i, j)),
        out_shape=jax.ShapeDtypeStruct((b_, t_, d), BF16),
        scratch_shapes=[pltpu.VMEM((tm, d), BF16)],
        compiler_params=_params(3, _vmem_limit(blocks, _nbytes((tm, d), BF16) + 3 * _nbytes((tm, tn), F32))),
        name="merge_branches",
    )(ya, yb, wa, wb, sgab, sgab)


def _merge2_kernel(u_ref, w_ref, b_ref, x_ref, g_ref, o_ref):
    y = jnp.dot(u_ref[0], w_ref[...], preferred_element_type=F32) + b_ref[...]
    o_ref[0] = x_ref[0] + g_ref[0] * y


def _merge2_call(u, wo, bo, x, gate, l, *, tm, tn):
    b_, t_, d = u.shape
    blocks = _nbytes((tm, d), BF16) + _nbytes((d, tn), BF16) + 2 * _nbytes((tm, tn), F32)
    return pl.pallas_call(
        _merge2_kernel,
        grid=(b_, t_ // tm, d // tn),
        in_specs=[pl.BlockSpec((1, tm, d), lambda b, i, j: (b, i, 0)),
                  pl.BlockSpec((None, d, tn), lambda b, i, j: (l, 0, j)),
                  pl.BlockSpec((None, 1, tn), lambda b, i, j: (l, 0, j)),
                  pl.BlockSpec((1, tm, tn), lambda b, i, j: (b, i, j)),
                  pl.BlockSpec((1, 1, tn), lambda b, i, j: (b, 0, j))],
        out_specs=pl.BlockSpec((1, tm, tn), lambda b, i, j: (b, i, j)),
        out_shape=jax.ShapeDtypeStruct((b_, t_, d), F32),
        compiler_params=_params(3, _vmem_limit(blocks, 2 * _nbytes((tm, tn), F32))),
        name="merge_out",
    )(u, wo, bo, x, gate)


def _in_offsets(d_model):
    a_width = A_HEADS * A_DV
    b_width = B_HEADS * B_DH
    sizes = (A_HEADS * A_DK, A_HEADS * A_DK, a_width, a_width, a_width, A_HEADS, A_HEADS, A_HEADS, A_HEADS,
             b_width, b_width, b_width, b_width, d_model, d_model)
    offs = np.concatenate([[0], np.cumsum(sizes)])
    return {n: (int(offs[i]), int(sizes[i])) for i, n in enumerate(IN_NAMES)}


WIDE_ORDER = ('aq', 'ak', 'av', 'ao', 'az', 'bq', 'bk', 'bv', 'bz', 'ga', 'gb')
GATE_ORDER = (('ai_f', 'ai_b'), ('af_f', 'af_b'))


def _prep_in_weights(w, b, na_q_g, na_k_g):
    offs = _in_offsets(w.shape[1])

    def seg(x, name):
        o, s = offs[name]
        return x[..., o:o + s]

    def layout(x):
        parts = [seg(x, n) for n in WIDE_ORDER]
        for names in GATE_ORDER:
            g = jnp.concatenate([seg(x, n) for n in names], axis=-1)
            parts.append(jnp.pad(g, [(0, 0)] * (g.ndim - 1) + [(0, LANES - g.shape[-1])]))
        return jnp.concatenate(parts, axis=-1)

    col, c0 = {}, 0
    for n in WIDE_ORDER:
        col[n] = c0
        c0 += offs[n][1]
    col['gates'] = c0
    scale = B_DH ** -0.5
    gain = jnp.concatenate([jnp.tile(na_q_g * scale, (1, B_HEADS)),
                            jnp.tile(na_k_g, (1, B_HEADS))], axis=-1)[:, None, :].astype(F32)
    return (layout(w).astype(BF16), layout(b)[:, None, :].astype(F32)), col, gain


def _rope_tables(n_tok, with_rope):
    nf = A_DK // 4
    kscale = A_DK ** -0.5
    if with_rope:
        t = jnp.arange(n_tok)
        rowp = (t // GRID_W).astype(F32)
        colp = (t % GRID_W).astype(F32)
        inv = ROPE_BASE ** (-jnp.arange(nf, dtype=F32) / nf)
        ar = rowp[:, None] * inv
        ac = colp[:, None] * inv
        cos = jnp.concatenate([jnp.cos(ar), jnp.cos(ar), jnp.cos(ac), jnp.cos(ac)], axis=1)
        sin = jnp.concatenate([-jnp.sin(ar), jnp.sin(ar), -jnp.sin(ac), jnp.sin(ac)], axis=1)
    else:
        cos = jnp.ones((n_tok, A_DK), F32)
        sin = jnp.zeros((n_tok, A_DK), F32)
    tq = jnp.stack([cos, sin])
    return jnp.stack([tq, tq * kscale])


def _na_bias_tables(rpb, rows):
    assert NA_QROWS == NA_KH // 2 and rows % NA_QROWS == 0 and rows >= 2 * NA_KH, (
        "one clamped query block at the top and one at the bottom needs 4-row blocks and >= 16 grid rows")
    n_dr = 2 * NA_KH - 1
    n_dc = 2 * NA_KW - 1
    qc = np.arange(GRID_W)
    kc = np.arange(GRID_W)
    ws = np.clip(qc - NA_KW // 2, 0, GRID_W - NA_KW)
    cvalid = (kc[None, :] >= ws[:, None]) & (kc[None, :] < ws[:, None] + NA_KW)
    dc = np.clip(kc[None, :] - qc[:, None] + NA_KW - 1, 0, n_dc - 1)
    csel = np.eye(n_dc, dtype=np.float32)[dc.reshape(-1)]
    rsel, valid = [], []
    for r in (0, NA_QROWS, rows - NA_QROWS):
        base = int(np.clip(r - NA_KH // 2, 0, rows - NA_KROWS))
        qr = r + np.arange(NA_QROWS)
        rs = np.clip(qr - NA_KH // 2, 0, rows - NA_KH)
        kr = base + np.arange(NA_KROWS)
        rvalid = (kr[None, :] >= rs[:, None]) & (kr[None, :] < rs[:, None] + NA_KH)
        dr = np.clip(kr[None, :] - qr[:, None] + NA_KH - 1, 0, n_dr - 1)
        rsel.append(np.eye(n_dr, dtype=np.float32)[dr.reshape(-1)])
        valid.append((rvalid[:, None, :, None] & cvalid[None, :, None, :]).reshape(NA_QROWS * GRID_W, NA_KROWS * GRID_W))
    rsel = jnp.asarray(np.stack(rsel))
    valid = jnp.asarray(np.stack(valid))
    hi = lax.Precision.HIGHEST
    t = jnp.einsum('xpu,lhuv->lhxpv', rsel, rpb.astype(F32), precision=hi)
    t = jnp.einsum('lhxpv,cv->lhxpc', t, jnp.asarray(csel), precision=hi)
    lead = t.shape[:3]
    t = t.reshape(lead + (NA_QROWS, NA_KROWS, GRID_W, GRID_W)).swapaxes(-3, -2)
    t = t.reshape(lead + (NA_QROWS * GRID_W, NA_KROWS * GRID_W))
    return jnp.where(valid, t, NEG_BIG)


def _project(h, wb, col, gain, l, rope_tab, *, tm):
    b_, t_, d = h.shape
    tn = 1024
    wq = A_HEADS * A_DK
    wv = A_HEADS * A_DV
    wh = B_HEADS * B_DH
    row_spec = pl.BlockSpec((1, tm, tn), lambda b, i, j: (b, i, j))
    hm_spec = pl.BlockSpec((1, tn // B_DH, tm, B_DH), lambda b, i, j: (b, j, i, 0))
    res = {}
    res['qk'] = _proj_call(
        h, wb, l, (col['aq'], 2 * wq), _ep_rope, [rope_tab],
        [pl.BlockSpec((None, 2, tm, A_DK), lambda b, i, j: (j, 0, i, 0))],
        jax.ShapeDtypeStruct((b_, t_, 2 * wq), BF16), row_spec, tm=tm, tn=tn, name="proj_qk_rope")
    res['av'] = _proj_call(
        h, wb, l, (col['av'], wv), _ep_plain, [], [],
        jax.ShapeDtypeStruct((b_, t_, wv), F32), row_spec, tm=tm, tn=tn, name="proj_av")
    res['og'] = _proj_call(
        h, wb, l, ((col['ao'], wv), (col['az'], wv)), _ep_outgate, [], [],
        jax.ShapeDtypeStruct((b_, t_, wv), F32),
        pl.BlockSpec((1, tm, tn // 2), lambda b, i, j: (b, i, j)), tm=tm, tn=tn // 2, name="proj_outgate")
    res['gates'] = _proj_call(
        h, wb, l, (col['gates'], 2 * LANES), _ep_plain, [], [],
        jax.ShapeDtypeStruct((b_, t_, 2 * LANES), F32),
        pl.BlockSpec((1, tm, 2 * LANES), lambda b, i, j: (b, i, j)), tm=tm, tn=2 * LANES, name="proj_gates")
    res['bqk'] = _proj_call(
        h, wb, l, (col['bq'], 2 * wh), _ep_headnorm, [gain],
        [pl.BlockSpec((None, 1, tn), lambda b, i, j: (l, 0, j))],
        jax.ShapeDtypeStruct((b_, 2 * B_HEADS, t_, B_DH), BF16), hm_spec, tm=tm, tn=tn, name="proj_bqk_norm")
    res['bv'] = _proj_call(
        h, wb, l, (col['bv'], wh), _ep_headmajor, [], [],
        jax.ShapeDtypeStruct((b_, B_HEADS, t_, B_DH), BF16), hm_spec, tm=tm, tn=tn, name="proj_bv")
    res['bz'] = _proj_call(
        h, wb, l, (col['bz'], wh), _ep_headmajor_silu, [], [],
        jax.ShapeDtypeStruct((b_, B_HEADS, t_, B_DH), F32), hm_spec, tm=tm, tn=tn, name="proj_bz_silu")
    res['gab'] = _proj_call(
        h, wb, l, (col['ga'], 2 * d), _ep_sigmoid, [], [],
        jax.ShapeDtypeStruct((b_, t_, 2 * d), F32), row_spec, tm=tm, tn=tn, name="proj_gab_sigmoid")
    return res


def kernel(x, c, ctx, c_ctx, w_mod, b_mod, norm_g, w_in, b_in, a_norm_g, w_br_a, w_br_b, na_q_g, na_k_g, na_rpb, w_out, b_out):
    bsz, n_tok, d = x.shape
    n_ctx = ctx.shape[1]
    depth = w_mod.shape[0]
    rows = n_tok // GRID_W
    assert n_tok % MLSTM_CHUNK == 0 and n_ctx % MLSTM_CHUNK == 0 and n_tok % (NA_QROWS * GRID_W) == 0
    mod_rows = 16
    assert bsz + 1 <= mod_rows
    tm_lat = min(1024, n_tok)
    tm_ctx = min(1024, n_ctx)
    te_lat = min(512, n_tok)
    te_ctx = min(512, n_ctx)

    rope_lat = _rope_tables(n_tok, True)
    rope_ctx = _rope_tables(n_ctx, False)
    w_in_b, col, qk_gain = _prep_in_weights(w_in, b_in, na_q_g, na_k_g)
    bias = _na_bias_tables(na_rpb, rows)
    w_mod_b = w_mod.astype(BF16)
    b_mod_r = b_mod[:, None, :]
    wa = w_br_a.astype(BF16)
    wb = w_br_b.astype(BF16)
    wo = w_out.astype(BF16)
    bo = b_out[:, None, :]
    a_g = a_norm_g[:, None, :]
    c_rows = jnp.zeros((mod_rows, d), F32).at[:bsz].set(c).at[bsz].set(c_ctx)
    zero_state = (jnp.zeros((bsz, 2 * A_HEADS, A_DK, A_DV), F32),
                  jnp.zeros((bsz, 2 * A_HEADS, 1, A_DK), F32),
                  jnp.zeros((bsz, 1, LANES), F32))

    def merge(ya, yb, sgab, x_in, gate, l, tm):
        u = _merge1_call(ya, yb, wa, wb, sgab, l, tm=tm, tn=512)
        return _merge2_call(u, wo, bo, x_in, gate, l, tm=tm, tn=1024)

    for l in range(depth):
        need_ctx = l < depth - 1
        mod = _mod_call(c_rows, w_mod_b, b_mod_r, l)
        shift, scale, gate = (mod[:bsz, k * d:(k + 1) * d].reshape(bsz, 1, d) for k in range(3))
        shift_c, scale_c, gate_c = (jnp.broadcast_to(mod[bsz, k * d:(k + 1) * d].reshape(1, 1, d), (bsz, 1, d))
                                    for k in range(3))
        g_row = norm_g[l].reshape(1, d)
        h_lat = _modulate_call(x, g_row, shift, scale, te_lat)
        h_ctx = _modulate_call(ctx, g_row, shift_c, scale_c, te_ctx)
        pl_ = _project(h_lat, w_in_b, col, qk_gain, l, rope_lat, tm=tm_lat)
        pc_ = _project(h_ctx, w_in_b, col, qk_gain, l, rope_ctx, tm=tm_ctx)

        hcf, hcb, c1, n1, m1 = _mlstm_call(pc_['qk'], pc_['av'], pc_['gates'], zero_state,
                                           chunk=MLSTM_CHUNK, dirs=(0, 1), name="mlstm_ctx")
        hlf = _mlstm_call(pl_['qk'], pl_['av'], pl_['gates'], (c1, n1, m1),
                          chunk=MLSTM_CHUNK, dirs=(0,), name="mlstm_lat_fwd")[0]
        ya_l = _mlstm_call(pl_['qk'], pl_['av'], pl_['gates'], (c1, n1, m1), chunk=MLSTM_CHUNK, dirs=(1,),
                           name="mlstm_lat_bwd", fuse=(hlf, pl_['og'], a_g, l))[0]

        yb_l = _na_call(pl_['bqk'], pl_['bv'], pc_['bqk'], pc_['bv'], pl_['bz'], bias, l)

        x = merge(ya_l, yb_l, pl_['gab'], x, gate, l, tm_lat)
        if need_ctx:
            ya_c = _mlstm_out_call(hcf, hcb, pc_['og'], a_g, l, te_ctx)
            yb_c = _ctx_attn_call(pc_['bqk'], pc_['bv'], pc_['bz'])
            ctx = merge(ya_c, yb_c, pc_['gab'], ctx, gate_c, l, tm_ctx)
    return x
```

```python
import functools

import numpy as np
import jax
import jax.numpy as jnp
from jax import lax
from jax.experimental import pallas as pl
from jax.experimental.pallas import tpu as pltpu

F32 = jnp.float32
BF16 = jnp.bfloat16

GRID_W = 64
A_HEADS = 8
A_DK = 128
A_DV = 256
B_HEADS = 16
B_DH = 128
NA_KH = 8
NA_KW = 16
ROPE_BASE = 10000.0
EPS = 1e-6
IN_NAMES = ('aq', 'ak', 'av', 'ao', 'az', 'ai_f', 'af_f', 'ai_b', 'af_b', 'bq', 'bk', 'bv', 'bz', 'ga', 'gb')

LANES = 128
SUBLANES = 8
VMEM_LIMIT_CAP = 56 * 1024 * 1024

MLSTM_CHUNK = 256
NA_QROWS = 4
NA_KROWS = NA_KH + NA_QROWS - 1
NA_UNROLL = 4
NEG_BIG = -1e30


def _vmem_limit(block_bytes, temp_bytes=0):
    need = 2 * block_bytes + temp_bytes + (4 << 20)
    return int(min(max(need, 16 << 20), VMEM_LIMIT_CAP))


def _nbytes(shape, dtype):
    return int(np.prod(shape)) * jnp.dtype(dtype).itemsize


def _params(n_grid, vmem):
    return pltpu.CompilerParams(dimension_semantics=("arbitrary",) * n_grid, vmem_limit_bytes=vmem)


def _sigmoid(x):
    return 1.0 / (1.0 + jnp.exp(-x))


def _mod_kernel(c_ref, w_ref, b_ref, o_ref):
    c = c_ref[...]
    o_ref[...] = jnp.dot((c * _sigmoid(c)).astype(BF16), w_ref[...], preferred_element_type=F32) + b_ref[...]


def _mod_call(c_rows, w, b, l):
    m, d = c_rows.shape
    n = w.shape[2]
    tn = 1536
    return pl.pallas_call(
        _mod_kernel,
        grid=(n // tn,),
        in_specs=[pl.BlockSpec((m, d), lambda j: (0, 0)),
                  pl.BlockSpec((None, d, tn), lambda j: (l, 0, j)),
                  pl.BlockSpec((None, 1, tn), lambda j: (l, 0, j))],
        out_specs=pl.BlockSpec((m, tn), lambda j: (0, j)),
        out_shape=jax.ShapeDtypeStruct((m, n), F32),
        compiler_params=_params(1, _vmem_limit(_nbytes((d, tn), BF16))),
        name="mod_dense",
    )(c_rows, w, b)


def _modulate_kernel(x_ref, g_ref, sh_ref, sc_ref, o_ref):
    x = x_ref[0]
    y = x * lax.rsqrt(jnp.mean(x * x, axis=-1, keepdims=True) + EPS) * g_ref[...]
    o_ref[0] = (y * (1.0 + sc_ref[0]) + sh_ref[0]).astype(BF16)


def _modulate_call(x, g, shift, scale, tm):
    b_, t_, d = x.shape
    return pl.pallas_call(
        _modulate_kernel,
        grid=(b_, t_ // tm),
        in_specs=[pl.BlockSpec((1, tm, d), lambda b, i: (b, i, 0)),
                  pl.BlockSpec((1, d), lambda b, i: (0, 0)),
                  pl.BlockSpec((1, 1, d), lambda b, i: (b, 0, 0)),
                  pl.BlockSpec((1, 1, d), lambda b, i: (b, 0, 0))],
        out_specs=pl.BlockSpec((1, tm, d), lambda b, i: (b, i, 0)),
        out_shape=jax.ShapeDtypeStruct((b_, t_, d), BF16),
        compiler_params=_params(2, _vmem_limit(_nbytes((tm, d), F32) + _nbytes((tm, d), BF16),
                                               2 * _nbytes((tm, d), F32))),
        name="modulate",
    )(x, g, shift, scale)


def _proj_kernel(a_ref, w_ref, b_ref, *rest, epilogue):
    acc = jnp.dot(a_ref[0], w_ref[...], preferred_element_type=F32) + b_ref[...]
    epilogue(acc, *rest)


def _proj_pair_kernel(a_ref, w0_ref, b0_ref, w1_ref, b1_ref, *rest, epilogue):
    a = a_ref[0]
    acc0 = jnp.dot(a, w0_ref[...], preferred_element_type=F32) + b0_ref[...]
    acc1 = jnp.dot(a, w1_ref[...], preferred_element_type=F32) + b1_ref[...]
    epilogue(acc0, acc1, *rest)


def _proj_call(a, wb, l, cols, epilogue, extras, extra_specs, out_shape, out_spec, *, tm, tn, name):
    w, bias = wb
    b_, t_, k = a.shape
    ranges = cols if isinstance(cols[0], tuple) else (cols,)
    n = ranges[0][1]
    w_specs, w_args = [], []
    for c0, cn in ranges:
        assert cn == n and c0 % tn == 0 and n % tn == 0
        jb = c0 // tn
        w_specs += [pl.BlockSpec((None, k, tn), lambda b, i, j, jb=jb: (l, 0, jb + j)),
                    pl.BlockSpec((None, 1, tn), lambda b, i, j, jb=jb: (l, 0, jb + j))]
        w_args += [w, bias]
    blocks = _nbytes((tm, k), BF16) + len(ranges) * _nbytes((k, tn), BF16) + 2 * _nbytes((tm, tn), F32)
    body = _proj_kernel if len(ranges) == 1 else _proj_pair_kernel
    return pl.pallas_call(
        functools.partial(body, epilogue=epilogue),
        grid=(b_, t_ // tm, n // tn),
        in_specs=[pl.BlockSpec((1, tm, k), lambda b, i, j: (b, i, 0))] + w_specs + list(extra_specs),
        out_specs=out_spec,
        out_shape=out_shape,
        compiler_params=_params(3, _vmem_limit(blocks, (2 + len(ranges)) * _nbytes((tm, tn), F32))),
        name=name,
    )(a, *w_args, *extras)


def _ep_plain(acc, o_ref):
    o_ref[0] = acc.astype(o_ref.dtype)


def _ep_sigmoid(acc, o_ref):
    o_ref[0] = _sigmoid(acc)


def _ep_rope(acc, tab_ref, o_ref):
    cos = tab_ref[0]
    sin = tab_ref[1]
    quarter = A_DK // 4
    lane = lax.broadcasted_iota(jnp.int32, cos.shape, 1)
    first_half = (lane % (2 * quarter)) < quarter
    for h in range(acc.shape[1] // A_DK):
        x = acc[:, h * A_DK:(h + 1) * A_DK]
        partner = jnp.where(first_half, pltpu.roll(x, A_DK - quarter, 1), pltpu.roll(x, quarter, 1))
        o_ref[0, :, h * A_DK:(h + 1) * A_DK] = (x * cos + partner * sin).astype(o_ref.dtype)


def _ep_outgate(acc_o, acc_z, o_ref):
    o_ref[0] = _sigmoid(acc_o) * (acc_z * _sigmoid(acc_z))


def _ep_headnorm(acc, gain_ref, o_ref):
    for h in range(acc.shape[1] // B_DH):
        x = acc[:, h * B_DH:(h + 1) * B_DH]
        y = x * lax.rsqrt(jnp.mean(x * x, axis=-1, keepdims=True) + EPS)
        o_ref[0, h] = (y * gain_ref[:, h * B_DH:(h + 1) * B_DH]).astype(o_ref.dtype)


def _ep_headmajor(acc, o_ref):
    for h in range(acc.shape[1] // B_DH):
        o_ref[0, h] = acc[:, h * B_DH:(h + 1) * B_DH].astype(o_ref.dtype)


def _ep_headmajor_silu(acc, o_ref):
    for h in range(acc.shape[1] // B_DH):
        x = acc[:, h * B_DH:(h + 1) * B_DH]
        o_ref[0, h] = (x * _sigmoid(x)).astype(o_ref.dtype)


def _running_max_rows(x, reverse):
    n = x.shape[0]
    rowi = lax.broadcasted_iota(jnp.int32, x.shape, 0)
    s = 1
    while s < n:
        if s < SUBLANES:
            if reverse:
                sh = jnp.where(rowi < n - s, pltpu.roll(x, n - s, 0), -jnp.inf)
            else:
                sh = jnp.where(rowi >= s, pltpu.roll(x, s, 0), -jnp.inf)
        else:
            pad = jnp.full((s, x.shape[1]), -jnp.inf, x.dtype)
            sh = jnp.concatenate([x[s:], pad], 0) if reverse else jnp.concatenate([pad, x[:n - s]], 0)
        x = jnp.maximum(x, sh)
        s *= 2
    return x


def _mlstm_kernel(*refs, chunk, dirs, fuse_out):
    L = chunk
    nd = len(dirs)
    in_refs = [refs[4 * i:4 * i + 4] for i in range(nd)]
    c0_ref, n0_ref, m0_ref = refs[4 * nd:4 * nd + 3]
    pos = 4 * nd + 3
    if fuse_out:
        hprev_ref, og_ref, gain_ref = refs[pos:pos + 3]
        pos += 3
    out_refs = refs[pos:pos + nd]
    c_ref, n_ref, m_ref = refs[pos + nd:pos + nd + 3]

    @pl.when(pl.program_id(1) == 0)
    def _():
        c_ref[...] = c0_ref[...]
        n_ref[...] = n0_ref[...]
        m_ref[...] = m0_ref[...]

    row = lax.broadcasted_iota(jnp.int32, (L, L), 0)
    col = lax.broadcasted_iota(jnp.int32, (L, L), 1)
    lane = lax.broadcasted_iota(jnp.int32, (1, LANES), 1)
    m_all = m_ref[0]
    m_next = m_all
    for d, (q_ref, k_ref, v_ref, g_ref), h_ref in zip(dirs, in_refs, out_refs):
        mask = (row >= col) if d == 0 else (row <= col)
        gates = g_ref[0]
        ig = gates[:, :LANES]
        fg = gates[:, LANES:]
        logf = -(jnp.maximum(-fg, 0.0) + jnp.log1p(jnp.exp(-jnp.abs(fg))))
        bcum = jnp.dot(mask.astype(F32), logf, precision=lax.Precision.HIGHEST, preferred_element_type=F32)
        a = ig - bcum
        m_run = jnp.maximum(_running_max_rows(a, reverse=(d == 1)), m_all)
        w_int_all = jnp.exp(m_all - m_run)
        e_mq_all = jnp.exp(-(bcum + m_run))
        last = L - 1 if d == 0 else 0
        m_last = m_run[last:last + 1]
        w_prev_all = jnp.exp(m_all - m_last)
        w_tok_all = jnp.exp(a - m_last)
        in_dir = (lane >= d * A_HEADS) & (lane < (d + 1) * A_HEADS)
        m_next = jnp.where(in_dir, bcum[last:last + 1] + m_last, m_next)
        a_t = a.T
        heads = range(A_HEADS)
        chs = [d * A_HEADS + h for h in heads]
        sls = [slice(h * A_DV, (h + 1) * A_DV) for h in heads]
        qs = [q_ref[0, :, h * A_DK:(h + 1) * A_DK] for h in heads]
        ks = [k_ref[0, :, h * A_DK:(h + 1) * A_DK] for h in heads]
        decay = [jnp.where(mask, jnp.exp(a_t[ch:ch + 1, :] - m_run[:, ch:ch + 1]), 0.0) for ch in chs]
        s_raw = [lax.dot_general(qs[h], ks[h], (((1,), (1,)), ((), ())), preferred_element_type=F32)
                 for h in heads]
        s_bf, rs, wrs = [], [], []
        for h in heads:
            ch = chs[h]
            s = s_raw[h] * decay[h]
            w_int = w_int_all[:, ch:ch + 1]
            den = (jnp.sum(s, axis=1, keepdims=True)
                   + w_int * jnp.sum(qs[h].astype(F32) * n_ref[0, ch], axis=1, keepdims=True))
            r = 1.0 / jnp.maximum(jnp.abs(den), e_mq_all[:, ch:ch + 1])
            s_bf.append(s.astype(BF16))
            rs.append(r)
            wrs.append(w_int * r)
        h_intra = [jnp.dot(s_bf[h], v_ref[0, :, sls[h]].astype(BF16), preferred_element_type=F32) for h in heads]
        h_inter = [jnp.dot(qs[h], c_ref[0, chs[h]].astype(BF16), preferred_element_type=F32) for h in heads]
        for h in heads:
            hv = h_intra[h] * rs[h] + h_inter[h] * wrs[h]
            if fuse_out:
                x = hprev_ref[0, :, sls[h]] + hv
                y = x * lax.rsqrt(jnp.mean(x * x, axis=-1, keepdims=True) + EPS) * gain_ref[:, sls[h]]
                h_ref[0, :, sls[h]] = (y * og_ref[0, :, sls[h]]).astype(h_ref.dtype)
            else:
                h_ref[0, :, sls[h]] = hv
        wv = [(w_tok_all[:, ch:ch + 1] * v_ref[0, :, sl]).astype(BF16) for ch, sl in zip(chs, sls)]
        c_upd = [lax.dot_general(ks[h], wv[h], (((0,), (0,)), ((), ())), preferred_element_type=F32)
                 for h in heads]
        for h in heads:
            ch = chs[h]
            w_prev = w_prev_all[:, ch:ch + 1]
            c_ref[0, ch] = w_prev * c_ref[0, ch] + c_upd[h]
            n_ref[0, ch] = (w_prev * n_ref[0, ch]
                            + jnp.sum(w_tok_all[:, ch:ch + 1] * ks[h].astype(F32), axis=0, keepdims=True))
    m_ref[0] = m_next


def _mlstm_call(qk, v, gates, state, *, chunk, dirs, name, fuse=None):
    b_, t_, _ = v.shape
    nc = t_ // chunk
    wq = A_HEADS * A_DK
    wv = A_HEADS * A_DV
    c0, n0, m0 = state

    def spec(width, blk, d):
        which = (lambda c: c) if d == 0 else (lambda c: nc - 1 - c)
        return pl.BlockSpec((1, chunk, width), lambda b, c: (b, which(c), blk))

    st_specs = [pl.BlockSpec((1, 2 * A_HEADS, A_DK, A_DV), lambda b, c: (b, 0, 0, 0)),
                pl.BlockSpec((1, 2 * A_HEADS, 1, A_DK), lambda b, c: (b, 0, 0, 0)),
                pl.BlockSpec((1, 1, LANES), lambda b, c: (b, 0, 0))]
    in_specs, args = [], []
    for d in dirs:
        in_specs += [spec(wq, 0, d), spec(wq, 1, d), spec(wv, 0, d), spec(2 * LANES, 0, d)]
        args += [qk, qk, v, gates]
    in_specs += st_specs
    args += [c0, n0, m0]
    out_dtype = F32
    nblk = len(dirs)
    if fuse is not None:
        assert len(dirs) == 1
        h_prev, og, gain, l = fuse
        in_specs += [spec(wv, 0, dirs[0]), spec(wv, 0, dirs[0]),
                     pl.BlockSpec((None, 1, wv), lambda b, c: (l, 0, 0))]
        args += [h_prev, og, gain]
        out_dtype = BF16
        nblk += 2
    blocks = (len(dirs) * (2 * _nbytes((chunk, wq), BF16) + _nbytes((chunk, 2 * LANES), F32))
              + (len(dirs) + nblk) * _nbytes((chunk, wv), F32) + 2 * _nbytes((2 * A_HEADS, A_DK, A_DV), F32))
    return pl.pallas_call(
        functools.partial(_mlstm_kernel, chunk=chunk, dirs=tuple(dirs), fuse_out=fuse is not None),
        grid=(b_, nc),
        in_specs=in_specs,
        out_specs=[spec(wv, 0, d) for d in dirs] + st_specs,
        out_shape=[jax.ShapeDtypeStruct((b_, t_, wv), out_dtype) for _ in dirs]
                  + [jax.ShapeDtypeStruct(c0.shape, F32), jax.ShapeDtypeStruct(n0.shape, F32),
                     jax.ShapeDtypeStruct(m0.shape, F32)],
        compiler_params=_params(2, _vmem_limit(blocks, 6 * A_HEADS * _nbytes((chunk, chunk), F32))),
        name=name,
    )(*args)


def _mlstm_out_kernel(hf_ref, hb_ref, og_ref, g_ref, o_ref):
    for h in range(A_HEADS):
        sl = slice(h * A_DV, (h + 1) * A_DV)
        x = hf_ref[0, :, sl] + hb_ref[0, :, sl]
        y = x * lax.rsqrt(jnp.mean(x * x, axis=-1, keepdims=True) + EPS) * g_ref[:, sl]
        o_ref[0, :, sl] = (y * og_ref[0, :, sl]).astype(BF16)


def _mlstm_out_call(hf, hb, og, g, l, tm):
    b_, t_, w = hf.shape
    spec = pl.BlockSpec((1, tm, w), lambda b, i: (b, i, 0))
    return pl.pallas_call(
        _mlstm_out_kernel,
        grid=(b_, t_ // tm),
        in_specs=[spec, spec, spec, pl.BlockSpec((None, 1, w), lambda b, i: (l, 0, 0))],
        out_specs=spec,
        out_shape=jax.ShapeDtypeStruct((b_, t_, w), BF16),
        compiler_params=_params(2, _vmem_limit(4 * _nbytes((tm, w), F32), 2 * _nbytes((tm, w), F32))),
        name="mlstm_out",
    )(hf, hb, og, g)


def _na_row_variants(rows):
    assert NA_QROWS == NA_KH // 2 and rows % NA_QROWS == 0 and rows >= 2 * NA_KH, (
        "one clamped query block at the top and one at the bottom needs 4-row blocks and >= 16 grid rows")
    out = []
    for r in (0, NA_QROWS, rows - NA_QROWS):
        base = min(max(r - NA_KH // 2, 0), rows - NA_KROWS)
        table = []
        for a in range(NA_QROWS):
            qr = r + a
            rs = min(max(qr - NA_KH // 2, 0), rows - NA_KH)
            table.append([(base + i) - qr + NA_KH - 1 if rs <= base + i < rs + NA_KH else None
                          for i in range(NA_KROWS)])
        out.append(table)
    return out


def _na_kernel(q_ref, k_ref, v_ref, kc_ref, vc_ref, z_ref, toep_ref, o_ref, bias_ref, *, rows, rows_per_step):
    nq = NA_QROWS * GRID_W
    nk = NA_KROWS * GRID_W
    r0 = pl.program_id(2) * rows_per_step
    kc = kc_ref[0, 0]
    vc = vc_ref[0, 0]

    @pl.when(pl.program_id(2) == 0)
    def _():
        masked = jnp.full((GRID_W, GRID_W), NEG_BIG, F32)
        for x, table in enumerate(_na_row_variants(rows)):
            for a in range(NA_QROWS):
                for i in range(NA_KROWS):
                    dr = table[a][i]
                    bias_ref[x, a * GRID_W:(a + 1) * GRID_W, i * GRID_W:(i + 1) * GRID_W] = (
                        masked if dr is None else toep_ref[dr])

    nb = min(NA_UNROLL, rows_per_step // NA_QROWS)

    def body(it, carry):
        blocks = range(nb)
        qs, ks, s_lat, s_ctx = [], [], [], []
        for i in blocks:
            blk = it * nb + i
            r = r0 + blk * NA_QROWS
            base = jnp.clip(r - NA_KH // 2, 0, rows - NA_KROWS)
            variant = jnp.where(r == 0, 0, jnp.where(r == rows - NA_QROWS, 2, 1))
            qs.append(pl.multiple_of(blk * nq, nq))
            ks.append(pl.multiple_of(base * GRID_W, GRID_W))
            q = q_ref[0, 0, pl.ds(qs[i], nq), :]
            k = k_ref[0, 0, pl.ds(ks[i], nk), :]
            s_lat.append(lax.dot_general(q, k, (((1,), (1,)), ((), ())), preferred_element_type=F32)
                         + bias_ref[variant])
            s_ctx.append(lax.dot_general(q, kc, (((1,), (1,)), ((), ())), preferred_element_type=F32))
        p_lat, p_ctx, rl = [], [], []
        for i in blocks:
            m = jnp.maximum(jnp.max(s_lat[i], axis=1, keepdims=True), jnp.max(s_ctx[i], axis=1, keepdims=True))
            p = jnp.exp(s_lat[i] - m)
            pc = jnp.exp(s_ctx[i] - m)
            rl.append(1.0 / (jnp.sum(p, axis=1, keepdims=True) + jnp.sum(pc, axis=1, keepdims=True)))
            p_lat.append(p.astype(BF16))
            p_ctx.append(pc.astype(BF16))
        o = [jnp.dot(p_lat[i], v_ref[0, 0, pl.ds(ks[i], nk), :], preferred_element_type=F32)
             + jnp.dot(p_ctx[i], vc, preferred_element_type=F32) for i in blocks]
        for i in blocks:
            o_ref[0, 0, pl.ds(qs[i], nq), :] = (o[i] * rl[i] * z_ref[0, 0, pl.ds(qs[i], nq), :]).astype(BF16)
        return carry

    lax.fori_loop(0, rows_per_step // (NA_QROWS * nb), body, 0)


def _na_call(qk, v, qk_ctx, v_ctx, z, toep, l):
    b_, _, t_, dh = v.shape
    tc = v_ctx.shape[2]
    rows = t_ // GRID_W
    rows_per_step = min(rows, 64)
    tq = rows_per_step * GRID_W
    nvar, nq, nk = 3, NA_QROWS * GRID_W, NA_KROWS * GRID_W
    n_dr = toep.shape[2]
    blocks = (2 * _nbytes((tq, dh), BF16) + 2 * _nbytes((t_, dh), BF16) + 2 * _nbytes((tc, dh), BF16)
              + _nbytes((tq, dh), F32) + _nbytes((n_dr, GRID_W, LANES), F32))
    return pl.pallas_call(
        functools.partial(_na_kernel, rows=rows, rows_per_step=rows_per_step),
        grid=(b_, B_HEADS, t_ // tq),
        in_specs=[pl.BlockSpec((1, 1, tq, dh), lambda b, h, i: (b, h, i, 0)),
                  pl.BlockSpec((1, 1, t_, dh), lambda b, h, i: (b, B_HEADS + h, 0, 0)),
                  pl.BlockSpec((1, 1, t_, dh), lambda b, h, i: (b, h, 0, 0)),
                  pl.BlockSpec((1, 1, tc, dh), lambda b, h, i: (b, B_HEADS + h, 0, 0)),
                  pl.BlockSpec((1, 1, tc, dh), lambda b, h, i: (b, h, 0, 0)),
                  pl.BlockSpec((1, 1, tq, dh), lambda b, h, i: (b, h, i, 0)),
                  pl.BlockSpec((None, None, n_dr, GRID_W, GRID_W), lambda b, h, i: (l, h, 0, 0, 0))],
        out_specs=pl.BlockSpec((1, 1, tq, dh), lambda b, h, i: (b, h, i, 0)),
        out_shape=jax.ShapeDtypeStruct((b_, B_HEADS, t_, dh), BF16),
        scratch_shapes=[pltpu.VMEM((nvar, nq, nk), F32)],
        compiler_params=_params(3, _vmem_limit(blocks, _nbytes((nvar, nq, nk), F32)
                                               + 8 * NA_UNROLL * _nbytes((nq, nk + tc), F32))),
        name="na_attention",
    )(qk, qk, v, qk_ctx, v_ctx, z, toep)


def _ctx_attn_kernel(q_ref, k_ref, v_ref, z_ref, o_ref):
    q = q_ref[0, 0]
    s = lax.dot_general(q, k_ref[0, 0], (((1,), (1,)), ((), ())), preferred_element_type=F32)
    m = jnp.max(s, axis=1, keepdims=True)
    p = jnp.exp(s - m)
    rl = 1.0 / jnp.sum(p, axis=1, keepdims=True)
    o = jnp.dot(p.astype(BF16), v_ref[0, 0], preferred_element_type=F32)
    o_ref[0, 0] = (o * rl * z_ref[0, 0]).astype(BF16)


def _ctx_attn_call(qk_ctx, v_ctx, z_ctx):
    b_, _, tc, dh = v_ctx.shape
    blk = (1, 1, tc, dh)
    return pl.pallas_call(
        _ctx_attn_kernel,
        grid=(b_, B_HEADS),
        in_specs=[pl.BlockSpec(blk, lambda b, h: (b, h, 0, 0)),
                  pl.BlockSpec(blk, lambda b, h: (b, B_HEADS + h, 0, 0)),
                  pl.BlockSpec(blk, lambda b, h: (b, h, 0, 0)),
                  pl.BlockSpec(blk, lambda b, h: (b, h, 0, 0))],
        out_specs=pl.BlockSpec(blk, lambda b, h: (b, h, 0, 0)),
        out_shape=jax.ShapeDtypeStruct((b_, B_HEADS, tc, dh), BF16),
        compiler_params=_params(2, _vmem_limit(5 * _nbytes((tc, dh), F32), 4 * _nbytes((tc, tc), F32))),
        name="ctx_attention",
    )(qk_ctx, qk_ctx, v_ctx, z_ctx)


def _merge1_kernel(ya_ref, yb_ref, wa_ref, wb_ref, sa_ref, sb_ref, o_ref, ybcat_ref):
    @pl.when(pl.program_id(2) == 0)
    def _():
        for h in range(B_HEADS):
            ybcat_ref[:, h * B_DH:(h + 1) * B_DH] = yb_ref[0, h]

    ua = jnp.dot(ya_ref[0], wa_ref[...], preferred_element_type=F32)
    ub = jnp.dot(ybcat_ref[...], wb_ref[...], preferred_element_type=F32)
    o_ref[0] = (sa_ref[0] * ua + sb_ref[0] * ub).astype(BF16)


def _merge1_call(ya, yb, wa, wb, sgab, l, *, tm, tn):
    b_, t_, d = ya.shape
    nj = d // tn
    blocks = (2 * _nbytes((tm, d), BF16) + 2 * _nbytes((d, tn), BF16) + 2 * _nbytes((tm, tn), F32)
              + _nbytes((tm, tn), BF16))
    return pl.pallas_call(
        _merge1_kernel,
        grid=(b_, t_ // tm, nj),
        in_specs=[pl.BlockSpec((1, tm, d), lambda b, i, j: (b, i, 0)),
                  pl.BlockSpec((1, B_HEADS, tm, B_DH), lambda b, i, j: (b, 0, i, 0)),
                  pl.BlockSpec((None, d, tn), lambda b, i, j: (l, 0, j)),
                  pl.BlockSpec((None, d, tn), lambda b, i, j: (l, 0, j)),
                  pl.BlockSpec((1, tm, tn), lambda b, i, j: (b, i, j)),
                  pl.BlockSpec((1, tm, tn), lambda b, i, j: (b, i, nj + j))],
        out_specs=pl.BlockSpec((1, tm, tn), lambda b, i, j: (b, i, j)),
        out_shape=jax.ShapeDtypeStruct((b_, t_, d), BF16),
        scratch_shapes=[pltpu.VMEM((tm, d), BF16)],
        compiler_params=_params(3, _vmem_limit(blocks, _nbytes((tm, d), BF16) + 3 * _nbytes((tm, tn), F32))),
        name="merge_branches",
    )(ya, yb, wa, wb, sgab, sgab)


def _merge2_kernel(u_ref, w_ref, b_ref, x_ref, g_ref, o_ref):
    y = jnp.dot(u_ref[0], w_ref[...], preferred_element_type=F32) + b_ref[...]
    o_ref[0] = x_ref[0] + g_ref[0] * y


def _merge2_call(u, wo, bo, x, gate, l, *, tm, tn):
    b_, t_, d = u.shape
    blocks = _nbytes((tm, d), BF16) + _nbytes((d, tn), BF16) + 2 * _nbytes((tm, tn), F32)
    return pl.pallas_call(
        _merge2_kernel,
        grid=(b_, t_ // tm, d // tn),
        in_specs=[pl.BlockSpec((1, tm, d), lambda b, i, j: (b, i, 0)),
                  pl.BlockSpec((None, d, tn), lambda b, i, j: (l, 0, j)),
                  pl.BlockSpec((None, 1, tn), lambda b, i, j: (l, 0, j)),
                  pl.BlockSpec((1, tm, tn), lambda b, i, j: (b, i, j)),
                  pl.BlockSpec((1, 1, tn), lambda b, i, j: (b, 0, j))],
        out_specs=pl.BlockSpec((1, tm, tn), lambda b, i, j: (b, i, j)),
        out_shape=jax.ShapeDtypeStruct((b_, t_, d), F32),
        compiler_params=_params(3, _vmem_limit(blocks, 2 * _nbytes((tm, tn), F32))),
        name="merge_out",
    )(u, wo, bo, x, gate)


def _in_offsets(d_model):
    a_width = A_HEADS * A_DV
    b_width = B_HEADS * B_DH
    sizes = (A_HEADS * A_DK, A_HEADS * A_DK, a_width, a_width, a_width, A_HEADS, A_HEADS, A_HEADS, A_HEADS,
             b_width, b_width, b_width, b_width, d_model, d_model)
    offs = np.concatenate([[0], np.cumsum(sizes)])
    return {n: (int(offs[i]), int(sizes[i])) for i, n in enumerate(IN_NAMES)}


WIDE_ORDER = ('aq', 'ak', 'av', 'ao', 'az', 'bq', 'bk', 'bv', 'bz', 'ga', 'gb')
GATE_ORDER = (('ai_f', 'ai_b'), ('af_f', 'af_b'))


def _prep_in_weights(w, b, na_q_g, na_k_g):
    offs = _in_offsets(w.shape[1])

    def seg(x, name):
        o, s = offs[name]
        return x[..., o:o + s]

    def layout(x):
        parts = [seg(x, n) for n in WIDE_ORDER]
        for names in GATE_ORDER:
            g = jnp.concatenate([seg(x, n) for n in names], axis=-1)
            parts.append(jnp.pad(g, [(0, 0)] * (g.ndim - 1) + [(0, LANES - g.shape[-1])]))
        return jnp.concatenate(parts, axis=-1)

    col, c0 = {}, 0
    for n in WIDE_ORDER:
        col[n] = c0
        c0 += offs[n][1]
    col['gates'] = c0
    scale = B_DH ** -0.5
    gain = jnp.concatenate([jnp.tile(na_q_g * scale, (1, B_HEADS)),
                            jnp.tile(na_k_g, (1, B_HEADS))], axis=-1)[:, None, :].astype(F32)
    return (layout(w).astype(BF16), layout(b)[:, None, :].astype(F32)), col, gain


def _rope_tables(n_tok, with_rope):
    nf = A_DK // 4
    kscale = A_DK ** -0.5
    if with_rope:
        t = jnp.arange(n_tok)
        rowp = (t // GRID_W).astype(F32)
        colp = (t % GRID_W).astype(F32)
        inv = ROPE_BASE ** (-jnp.arange(nf, dtype=F32) / nf)
        ar = rowp[:, None] * inv
        ac = colp[:, None] * inv
        cos = jnp.concatenate([jnp.cos(ar), jnp.cos(ar), jnp.cos(ac), jnp.cos(ac)], axis=1)
        sin = jnp.concatenate([-jnp.sin(ar), jnp.sin(ar), -jnp.sin(ac), jnp.sin(ac)], axis=1)
    else:
        cos = jnp.ones((n_tok, A_DK), F32)
        sin = jnp.zeros((n_tok, A_DK), F32)
    tq = jnp.stack([cos, sin])
    return jnp.stack([tq, tq * kscale])


def _na_column_tables(rpb):
    n_dc = 2 * NA_KW - 1
    qc = np.arange(GRID_W)
    kc = np.arange(GRID_W)
    ws = np.clip(qc - NA_KW // 2, 0, GRID_W - NA_KW)
    cvalid = (kc[None, :] >= ws[:, None]) & (kc[None, :] < ws[:, None] + NA_KW)
    dc = np.clip(kc[None, :] - qc[:, None] + NA_KW - 1, 0, n_dc - 1)
    csel = np.eye(n_dc, dtype=np.float32)[dc.reshape(-1)]
    t = jnp.einsum('lhuv,cv->lhuc', rpb.astype(F32), jnp.asarray(csel), precision=lax.Precision.HIGHEST)
    t = t.reshape(t.shape[:3] + (GRID_W, GRID_W))
    return jnp.where(jnp.asarray(cvalid), t, NEG_BIG)


def _project(h, wb, col, gain, l, rope_tab, *, tm):
    b_, t_, d = h.shape
    tn = 1024
    wq = A_HEADS * A_DK
    wv = A_HEADS * A_DV
    wh = B_HEADS * B_DH
    row_spec = pl.BlockSpec((1, tm, tn), lambda b, i, j: (b, i, j))
    hm_spec = pl.BlockSpec((1, tn // B_DH, tm, B_DH), lambda b, i, j: (b, j, i, 0))
    res = {}
    res['qk'] = _proj_call(
        h, wb, l, (col['aq'], 2 * wq), _ep_rope, [rope_tab],
        [pl.BlockSpec((None, 2, tm, A_DK), lambda b, i, j: (j, 0, i, 0))],
        jax.ShapeDtypeStruct((b_, t_, 2 * wq), BF16), row_spec, tm=tm, tn=tn, name="proj_qk_rope")
    res['av'] = _proj_call(
        h, wb, l, (col['av'], wv), _ep_plain, [], [],
        jax.ShapeDtypeStruct((b_, t_, wv), F32), row_spec, tm=tm, tn=tn, name="proj_av")
    res['og'] = _proj_call(
        h, wb, l, ((col['ao'], wv), (col['az'], wv)), _ep_outgate, [], [],
        jax.ShapeDtypeStruct((b_, t_, wv), F32),
        pl.BlockSpec((1, tm, tn // 2), lambda b, i, j: (b, i, j)), tm=tm, tn=tn // 2, name="proj_outgate")
    res['gates'] = _proj_call(
        h, wb, l, (col['gates'], 2 * LANES), _ep_plain, [], [],
        jax.ShapeDtypeStruct((b_, t_, 2 * LANES), F32),
        pl.BlockSpec((1, tm, 2 * LANES), lambda b, i, j: (b, i, j)), tm=tm, tn=2 * LANES, name="proj_gates")
    res['bqk'] = _proj_call(
        h, wb, l, (col['bq'], 2 * wh), _ep_headnorm, [gain],
        [pl.BlockSpec((None, 1, tn), lambda b, i, j: (l, 0, j))],
        jax.ShapeDtypeStruct((b_, 2 * B_HEADS, t_, B_DH), BF16), hm_spec, tm=tm, tn=tn, name="proj_bqk_norm")
    res['bv'] = _proj_call(
        h, wb, l, (col['bv'], wh), _ep_headmajor, [], [],
        jax.ShapeDtypeStruct((b_, B_HEADS, t_, B_DH), BF16), hm_spec, tm=tm, tn=tn, name="proj_bv")
    res['bz'] = _proj_call(
        h, wb, l, (col['bz'], wh), _ep_headmajor_silu, [], [],
        jax.ShapeDtypeStruct((b_, B_HEADS, t_, B_DH), F32), hm_spec, tm=tm, tn=tn, name="proj_bz_silu")
    res['gab'] = _proj_call(
        h, wb, l, (col['ga'], 2 * d), _ep_sigmoid, [], [],
        jax.ShapeDtypeStruct((b_, t_, 2 * d), F32), row_spec, tm=tm, tn=tn, name="proj_gab_sigmoid")
    return res


def kernel(x, c, ctx, c_ctx, w_mod, b_mod, norm_g, w_in, b_in, a_norm_g, w_br_a, w_br_b, na_q_g, na_k_g, na_rpb, w_out, b_out):
    bsz, n_tok, d = x.shape
    n_ctx = ctx.shape[1]
    depth = w_mod.shape[0]
    rows = n_tok // GRID_W
    assert n_tok % MLSTM_CHUNK == 0 and n_ctx % MLSTM_CHUNK == 0 and n_tok % (NA_QROWS * GRID_W) == 0
    mod_rows = 16
    assert bsz + 1 <= mod_rows
    tm_lat = min(1024, n_tok)
    tm_ctx = min(1024, n_ctx)
    te_lat = min(512, n_tok)
    te_ctx = min(512, n_ctx)

    rope_lat = _rope_tables(n_tok, True)
    rope_ctx = _rope_tables(n_ctx, False)
    w_in_b, col, qk_gain = _prep_in_weights(w_in, b_in, na_q_g, na_k_g)
    bias = _na_column_tables(na_rpb)
    w_mod_b = w_mod.astype(BF16)
    b_mod_r = b_mod[:, None, :]
    wa = w_br_a.astype(BF16)
    wb = w_br_b.astype(BF16)
    wo = w_out.astype(BF16)
    bo = b_out[:, None, :]
    a_g = a_norm_g[:, None, :]
    c_rows = jnp.zeros((mod_rows, d), F32).at[:bsz].set(c).at[bsz].set(c_ctx)
    zero_state = (jnp.zeros((bsz, 2 * A_HEADS, A_DK, A_DV), F32),
                  jnp.zeros((bsz, 2 * A_HEADS, 1, A_DK), F32),
                  jnp.zeros((bsz, 1, LANES), F32))

    def merge(ya, yb, sgab, x_in, gate, l, tm):
        u = _merge1_call(ya, yb, wa, wb, sgab, l, tm=tm, tn=512)
        return _merge2_call(u, wo, bo, x_in, gate, l, tm=tm, tn=1024)

    for l in range(depth):
        need_ctx = l < depth - 1
        mod = _mod_call(c_rows, w_mod_b, b_mod_r, l)
        shift, scale, gate = (mod[:bsz, k * d:(k + 1) * d].reshape(bsz, 1, d) for k in range(3))
        shift_c, scale_c, gate_c = (jnp.broadcast_to(mod[bsz, k * d:(k + 1) * d].reshape(1, 1, d), (bsz, 1, d))
                                    for k in range(3))
        g_row = norm_g[l].reshape(1, d)
        h_lat = _modulate_call(x, g_row, shift, scale, te_lat)
        h_ctx = _modulate_call(ctx, g_row, shift_c, scale_c, te_ctx)
        pl_ = _project(h_lat, w_in_b, col, qk_gain, l, rope_lat, tm=tm_lat)
        pc_ = _project(h_ctx, w_in_b, col, qk_gain, l, rope_ctx, tm=tm_ctx)

        hcf, hcb, c1, n1, m1 = _mlstm_call(pc_['qk'], pc_['av'], pc_['gates'], zero_state,
                                           chunk=MLSTM_CHUNK, dirs=(0, 1), name="mlstm_ctx")
        hlf = _mlstm_call(pl_['qk'], pl_['av'], pl_['gates'], (c1, n1, m1),
                          chunk=MLSTM_CHUNK, dirs=(0,), name="mlstm_lat_fwd")[0]
        ya_l = _mlstm_call(pl_['qk'], pl_['av'], pl_['gates'], (c1, n1, m1), chunk=MLSTM_CHUNK, dirs=(1,),
                           name="mlstm_lat_bwd", fuse=(hlf, pl_['og'], a_g, l))[0]

        yb_l = _na_call(pl_['bqk'], pl_['bv'], pc_['bqk'], pc_['bv'], pl_['bz'], bias, l)

        x = merge(ya_l, yb_l, pl_['gab'], x, gate, l, tm_lat)
        if need_ctx:
            ya_c = _mlstm_out_call(hcf, hcb, pc_['og'], a_g, l, te_ctx)
            yb_c = _ctx_attn_call(pc_['bqk'], pc_['bv'], pc_['bz'])
            ctx = merge(ya_c, yb_c, pc_['gab'], ctx, gate_c, l, tm_ctx)
    return x
```

```python
import functools

import numpy as np
import jax
import jax.numpy as jnp
from jax import lax
from jax.experimental import pallas as pl
from jax.experimental.pallas import tpu as pltpu

F32 = jnp.float32
BF16 = jnp.bfloat16

GRID_W = 64
A_HEADS = 8
A_DK = 128
A_DV = 256
B_HEADS = 16
B_DH = 128
NA_KH = 8
NA_KW = 16
ROPE_BASE = 10000.0
EPS = 1e-6
IN_NAMES = ('aq', 'ak', 'av', 'ao', 'az', 'ai_f', 'af_f', 'ai_b', 'af_b', 'bq', 'bk', 'bv', 'bz', 'ga', 'gb')

LANES = 128
SUBLANES = 8
VMEM_LIMIT_CAP = 56 * 1024 * 1024

MLSTM_CHUNK = 256
NA_QROWS = 4
NA_KROWS = NA_KH + NA_QROWS - 1
NA_UNROLL = 4
NEG_BIG = -1e30


def _vmem_limit(block_bytes, temp_bytes=0):
    need = 2 * block_bytes + temp_bytes + (4 << 20)
    return int(min(max(need, 16 << 20), VMEM_LIMIT_CAP))


def _nbytes(shape, dtype):
    return int(np.prod(shape)) * jnp.dtype(dtype).itemsize


def _params(n_grid, vmem):
    return pltpu.CompilerParams(dimension_semantics=("arbitrary",) * n_grid, vmem_limit_bytes=vmem)


def _sigmoid(x):
    return 1.0 / (1.0 + jnp.exp(-x))


def _mod_kernel(c_ref, w_ref, b_ref, o_ref):
    c = c_ref[...]
    o_ref[...] = jnp.dot((c * _sigmoid(c)).astype(BF16), w_ref[...], preferred_element_type=F32) + b_ref[...]


def _mod_call(c_rows, w, b, l):
    m, d = c_rows.shape
    n = w.shape[2]
    tn = 1536
    return pl.pallas_call(
        _mod_kernel,
        grid=(n // tn,),
        in_specs=[pl.BlockSpec((m, d), lambda j: (0, 0)),
                  pl.BlockSpec((None, d, tn), lambda j: (l, 0, j)),
                  pl.BlockSpec((None, 1, tn), lambda j: (l, 0, j))],
        out_specs=pl.BlockSpec((m, tn), lambda j: (0, j)),
        out_shape=jax.ShapeDtypeStruct((m, n), F32),
        compiler_params=_params(1, _vmem_limit(_nbytes((d, tn), BF16))),
        name="mod_dense",
    )(c_rows, w, b)


def _modulate_kernel(x_ref, g_ref, sh_ref, sc_ref, o_ref):
    x = x_ref[0]
    y = x * lax.rsqrt(jnp.mean(x * x, axis=-1, keepdims=True) + EPS) * g_ref[...]
    o_ref[0] = (y * (1.0 + sc_ref[0]) + sh_ref[0]).astype(BF16)


def _modulate_call(x, g, shift, scale, tm):
    b_, t_, d = x.shape
    return pl.pallas_call(
        _modulate_kernel,
        grid=(b_, t_ // tm),
        in_specs=[pl.BlockSpec((1, tm, d), lambda b, i: (b, i, 0)),
                  pl.BlockSpec((1, d), lambda b, i: (0, 0)),
                  pl.BlockSpec((1, 1, d), lambda b, i: (b, 0, 0)),
                  pl.BlockSpec((1, 1, d), lambda b, i: (b, 0, 0))],
        out_specs=pl.BlockSpec((1, tm, d), lambda b, i: (b, i, 0)),
        out_shape=jax.ShapeDtypeStruct((b_, t_, d), BF16),
        compiler_params=_params(2, _vmem_limit(_nbytes((tm, d), F32) + _nbytes((tm, d), BF16),
                                               2 * _nbytes((tm, d), F32))),
        name="modulate",
    )(x, g, shift, scale)


def _proj_kernel(a_ref, w_ref, b_ref, *rest, epilogue):
    acc = jnp.dot(a_ref[0], w_ref[...], preferred_element_type=F32) + b_ref[...]
    epilogue(acc, *rest)


def _proj_pair_kernel(a_ref, w0_ref, b0_ref, w1_ref, b1_ref, *rest, epilogue):
    a = a_ref[0]
    acc0 = jnp.dot(a, w0_ref[...], preferred_element_type=F32) + b0_ref[...]
    acc1 = jnp.dot(a, w1_ref[...], preferred_element_type=F32) + b1_ref[...]
    epilogue(acc0, acc1, *rest)


def _proj_call(a, wb, l, cols, epilogue, extras, extra_specs, out_shape, out_spec, *, tm, tn, name):
    w, bias = wb
    b_, t_, k = a.shape
    ranges = cols if isinstance(cols[0], tuple) else (cols,)
    n = ranges[0][1]
    w_specs, w_args = [], []
    for c0, cn in ranges:
        assert cn == n and c0 % tn == 0 and n % tn == 0
        jb = c0 // tn
        w_specs += [pl.BlockSpec((None, k, tn), lambda b, i, j, jb=jb: (l, 0, jb + j)),
                    pl.BlockSpec((None, 1, tn), lambda b, i, j, jb=jb: (l, 0, jb + j))]
        w_args += [w, bias]
    blocks = _nbytes((tm, k), BF16) + len(ranges) * _nbytes((k, tn), BF16) + 2 * _nbytes((tm, tn), F32)
    body = _proj_kernel if len(ranges) == 1 else _proj_pair_kernel
    return pl.pallas_call(
        functools.partial(body, epilogue=epilogue),
        grid=(b_, t_ // tm, n // tn),
        in_specs=[pl.BlockSpec((1, tm, k), lambda b, i, j: (b, i, 0))] + w_specs + list(extra_specs),
        out_specs=out_spec,
        out_shape=out_shape,
        compiler_params=_params(3, _vmem_limit(blocks, (2 + len(ranges)) * _nbytes((tm, tn), F32))),
        name=name,
    )(a, *w_args, *extras)


def _ep_plain(acc, o_ref):
    o_ref[0] = acc.astype(o_ref.dtype)


def _ep_sigmoid(acc, o_ref):
    o_ref[0] = _sigmoid(acc)


def _ep_rope(acc, tab_ref, o_ref):
    cos = tab_ref[0]
    sin = tab_ref[1]
    quarter = A_DK // 4
    lane = lax.broadcasted_iota(jnp.int32, cos.shape, 1)
    first_half = (lane % (2 * quarter)) < quarter
    for h in range(acc.shape[1] // A_DK):
        x = acc[:, h * A_DK:(h + 1) * A_DK]
        partner = jnp.where(first_half, pltpu.roll(x, A_DK - quarter, 1), pltpu.roll(x, quarter, 1))
        o_ref[0, :, h * A_DK:(h + 1) * A_DK] = (x * cos + partner * sin).astype(o_ref.dtype)


def _ep_outgate(acc_o, acc_z, o_ref):
    o_ref[0] = _sigmoid(acc_o) * (acc_z * _sigmoid(acc_z))


def _ep_headnorm(acc, gain_ref, o_ref):
    for h in range(acc.shape[1] // B_DH):
        x = acc[:, h * B_DH:(h + 1) * B_DH]
        y = x * lax.rsqrt(jnp.mean(x * x, axis=-1, keepdims=True) + EPS)
        o_ref[0, h] = (y * gain_ref[:, h * B_DH:(h + 1) * B_DH]).astype(o_ref.dtype)


def _ep_headmajor(acc, o_ref):
    for h in range(acc.shape[1] // B_DH):
        o_ref[0, h] = acc[:, h * B_DH:(h + 1) * B_DH].astype(o_ref.dtype)


def _ep_headmajor_silu(acc, o_ref):
    for h in range(acc.shape[1] // B_DH):
        x = acc[:, h * B_DH:(h + 1) * B_DH]
        o_ref[0, h] = (x * _sigmoid(x)).astype(o_ref.dtype)


def _running_max_rows(x, reverse):
    n = x.shape[0]
    rowi = lax.broadcasted_iota(jnp.int32, x.shape, 0)
    s = 1
    while s < n:
        if s < SUBLANES:
            if reverse:
                sh = jnp.where(rowi < n - s, pltpu.roll(x, n - s, 0), -jnp.inf)
            else:
                sh = jnp.where(rowi >= s, pltpu.roll(x, s, 0), -jnp.inf)
        else:
            pad = jnp.full((s, x.shape[1]), -jnp.inf, x.dtype)
            sh = jnp.concatenate([x[s:], pad], 0) if reverse else jnp.concatenate([pad, x[:n - s]], 0)
        x = jnp.maximum(x, sh)
        s *= 2
    return x


def _mlstm_kernel(*refs, chunk, dirs, fuse_out):
    L = chunk
    nd = len(dirs)
    in_refs = [refs[4 * i:4 * i + 4] for i in range(nd)]
    c0_ref, n0_ref, m0_ref = refs[4 * nd:4 * nd + 3]
    pos = 4 * nd + 3
    if fuse_out:
        hprev_ref, og_ref, gain_ref = refs[pos:pos + 3]
        pos += 3
    out_refs = refs[pos:pos + nd]
    c_ref, n_ref, m_ref = refs[pos + nd:pos + nd + 3]

    @pl.when(pl.program_id(1) == 0)
    def _():
        c_ref[...] = c0_ref[...]
        n_ref[...] = n0_ref[...]
        m_ref[...] = m0_ref[...]

    row = lax.broadcasted_iota(jnp.int32, (L, L), 0)
    col = lax.broadcasted_iota(jnp.int32, (L, L), 1)
    lane = lax.broadcasted_iota(jnp.int32, (1, LANES), 1)
    m_all = m_ref[0]
    m_next = m_all
    for d, (q_ref, k_ref, v_ref, g_ref), h_ref in zip(dirs, in_refs, out_refs):
        mask = (row >= col) if d == 0 else (row <= col)
        gates = g_ref[0]
        ig = gates[:, :LANES]
        fg = gates[:, LANES:]
        logf = -(jnp.maximum(-fg, 0.0) + jnp.log1p(jnp.exp(-jnp.abs(fg))))
        bcum = jnp.dot(mask.astype(F32), logf, precision=lax.Precision.HIGHEST, preferred_element_type=F32)
        a = ig - bcum
        m_run = jnp.maximum(_running_max_rows(a, reverse=(d == 1)), m_all)
        w_int_all = jnp.exp(m_all - m_run)
        e_mq_all = jnp.exp(-(bcum + m_run))
        last = L - 1 if d == 0 else 0
        m_last = m_run[last:last + 1]
        w_prev_all = jnp.exp(m_all - m_last)
        w_tok_all = jnp.exp(a - m_last)
        in_dir = (lane >= d * A_HEADS) & (lane < (d + 1) * A_HEADS)
        m_next = jnp.where(in_dir, bcum[last:last + 1] + m_last, m_next)
        a_t = a.T
        heads = range(A_HEADS)
        chs = [d * A_HEADS + h for h in heads]
        sls = [slice(h * A_DV, (h + 1) * A_DV) for h in heads]
        qs = [q_ref[0, :, h * A_DK:(h + 1) * A_DK] for h in heads]
        ks = [k_ref[0, :, h * A_DK:(h + 1) * A_DK] for h in heads]
        decay = [jnp.where(mask, jnp.exp(a_t[ch:ch + 1, :] - m_run[:, ch:ch + 1]), 0.0) for ch in chs]
        s_raw = [lax.dot_general(qs[h], ks[h], (((1,), (1,)), ((), ())), preferred_element_type=F32)
                 for h in heads]
        s_dec = [s_raw[h] * decay[h] for h in heads]
        s_bf = [s.astype(BF16) for s in s_dec]
        qn = [jnp.sum(qs[h].astype(F32) * n_ref[0, chs[h]], axis=1, keepdims=True) for h in heads]
        s_sum = [jnp.sum(s, axis=1, keepdims=True) for s in s_dec]
        rs, wrs = [], []
        for h in heads:
            ch = chs[h]
            w_int = w_int_all[:, ch:ch + 1]
            r = 1.0 / jnp.maximum(jnp.abs(s_sum[h] + w_int * qn[h]), e_mq_all[:, ch:ch + 1])
            rs.append(r)
            wrs.append(w_int * r)
        h_intra = [jnp.dot(s_bf[h], v_ref[0, :, sls[h]].astype(BF16), preferred_element_type=F32) for h in heads]
        h_inter = [jnp.dot(qs[h], c_ref[0, chs[h]].astype(BF16), preferred_element_type=F32) for h in heads]
        hv = [h_intra[h] * rs[h] + h_inter[h] * wrs[h] for h in heads]
        if fuse_out:
            xs = [hprev_ref[0, :, sls[h]] + hv[h] for h in heads]
            inv = [lax.rsqrt(jnp.mean(x * x, axis=-1, keepdims=True) + EPS) for x in xs]
            for h in heads:
                y = xs[h] * inv[h] * gain_ref[:, sls[h]]
                h_ref[0, :, sls[h]] = (y * og_ref[0, :, sls[h]]).astype(h_ref.dtype)
        else:
            for h in heads:
                h_ref[0, :, sls[h]] = hv[h]
        wv = [(w_tok_all[:, ch:ch + 1] * v_ref[0, :, sl]).astype(BF16) for ch, sl in zip(chs, sls)]
        c_upd = [lax.dot_general(ks[h], wv[h], (((0,), (0,)), ((), ())), preferred_element_type=F32)
                 for h in heads]
        for h in heads:
            ch = chs[h]
            w_prev = w_prev_all[:, ch:ch + 1]
            c_ref[0, ch] = w_prev * c_ref[0, ch] + c_upd[h]
            n_ref[0, ch] = (w_prev * n_ref[0, ch]
                            + jnp.sum(w_tok_all[:, ch:ch + 1] * ks[h].astype(F32), axis=0, keepdims=True))
    m_ref[0] = m_next


def _mlstm_call(qk, v, gates, state, *, chunk, dirs, name, fuse=None):
    b_, t_, _ = v.shape
    nc = t_ // chunk
    wq = A_HEADS * A_DK
    wv = A_HEADS * A_DV
    c0, n0, m0 = state

    def spec(width, blk, d):
        which = (lambda c: c) if d == 0 else (lambda c: nc - 1 - c)
        return pl.BlockSpec((1, chunk, width), lambda b, c: (b, which(c), blk))

    st_specs = [pl.BlockSpec((1, 2 * A_HEADS, A_DK, A_DV), lambda b, c: (b, 0, 0, 0)),
                pl.BlockSpec((1, 2 * A_HEADS, 1, A_DK), lambda b, c: (b, 0, 0, 0)),
                pl.BlockSpec((1, 1, LANES), lambda b, c: (b, 0, 0))]
    in_specs, args = [], []
    for d in dirs:
        in_specs += [spec(wq, 0, d), spec(wq, 1, d), spec(wv, 0, d), spec(2 * LANES, 0, d)]
        args += [qk, qk, v, gates]
    in_specs += st_specs
    args += [c0, n0, m0]
    out_dtype = F32
    nblk = len(dirs)
    if fuse is not None:
        assert len(dirs) == 1
        h_prev, og, gain, l = fuse
        in_specs += [spec(wv, 0, dirs[0]), spec(wv, 0, dirs[0]),
                     pl.BlockSpec((None, 1, wv), lambda b, c: (l, 0, 0))]
        args += [h_prev, og, gain]
        out_dtype = BF16
        nblk += 2
    blocks = (len(dirs) * (2 * _nbytes((chunk, wq), BF16) + _nbytes((chunk, 2 * LANES), F32))
              + (len(dirs) + nblk) * _nbytes((chunk, wv), F32) + 2 * _nbytes((2 * A_HEADS, A_DK, A_DV), F32))
    return pl.pallas_call(
        functools.partial(_mlstm_kernel, chunk=chunk, dirs=tuple(dirs), fuse_out=fuse is not None),
        grid=(b_, nc),
        in_specs=in_specs,
        out_specs=[spec(wv, 0, d) for d in dirs] + st_specs,
        out_shape=[jax.ShapeDtypeStruct((b_, t_, wv), out_dtype) for _ in dirs]
                  + [jax.ShapeDtypeStruct(c0.shape, F32), jax.ShapeDtypeStruct(n0.shape, F32),
                     jax.ShapeDtypeStruct(m0.shape, F32)],
        compiler_params=_params(2, _vmem_limit(blocks, 6 * A_HEADS * _nbytes((chunk, chunk), F32))),
        name=name,
    )(*args)


def _mlstm_out_kernel(hf_ref, hb_ref, og_ref, g_ref, o_ref):
    for h in range(A_HEADS):
        sl = slice(h * A_DV, (h + 1) * A_DV)
        x = hf_ref[0, :, sl] + hb_ref[0, :, sl]
        y = x * lax.rsqrt(jnp.mean(x * x, axis=-1, keepdims=True) + EPS) * g_ref[:, sl]
        o_ref[0, :, sl] = (y * og_ref[0, :, sl]).astype(BF16)


def _mlstm_out_call(hf, hb, og, g, l, tm):
    b_, t_, w = hf.shape
    spec = pl.BlockSpec((1, tm, w), lambda b, i: (b, i, 0))
    return pl.pallas_call(
        _mlstm_out_kernel,
        grid=(b_, t_ // tm),
        in_specs=[spec, spec, spec, pl.BlockSpec((None, 1, w), lambda b, i: (l, 0, 0))],
        out_specs=spec,
        out_shape=jax.ShapeDtypeStruct((b_, t_, w), BF16),
        compiler_params=_params(2, _vmem_limit(4 * _nbytes((tm, w), F32), 2 * _nbytes((tm, w), F32))),
        name="mlstm_out",
    )(hf, hb, og, g)


def _na_row_variants(rows):
    assert NA_QROWS == NA_KH // 2 and rows % NA_QROWS == 0 and rows >= 2 * NA_KH, (
        "one clamped query block at the top and one at the bottom needs 4-row blocks and >= 16 grid rows")
    out = []
    for r in (0, NA_QROWS, rows - NA_QROWS):
        base = min(max(r - NA_KH // 2, 0), rows - NA_KROWS)
        table = []
        for a in range(NA_QROWS):
            qr = r + a
            rs = min(max(qr - NA_KH // 2, 0), rows - NA_KH)
            table.append([(base + i) - qr + NA_KH - 1 if rs <= base + i < rs + NA_KH else None
                          for i in range(NA_KROWS)])
        out.append(table)
    return out


def _na_kernel(q_ref, k_ref, v_ref, kc_ref, vc_ref, z_ref, toep_ref, o_ref, bias_ref, *, rows, rows_per_step):
    nq = NA_QROWS * GRID_W
    nk = NA_KROWS * GRID_W
    r0 = pl.program_id(2) * rows_per_step
    kc = kc_ref[0, 0]
    vc = vc_ref[0, 0]

    @pl.when(pl.program_id(2) == 0)
    def _():
        masked = jnp.full((GRID_W, GRID_W), NEG_BIG, F32)
        for x, table in enumerate(_na_row_variants(rows)):
            for a in range(NA_QROWS):
                for i in range(NA_KROWS):
                    dr = table[a][i]
                    bias_ref[x, a * GRID_W:(a + 1) * GRID_W, i * GRID_W:(i + 1) * GRID_W] = (
                        masked if dr is None else toep_ref[dr])

    nb = min(NA_UNROLL, rows_per_step // NA_QROWS)

    def body(it, carry):
        blocks = range(nb)
        qs, ks, s_lat, s_ctx = [], [], [], []
        for i in blocks:
            blk = it * nb + i
            r = r0 + blk * NA_QROWS
            base = jnp.clip(r - NA_KH // 2, 0, rows - NA_KROWS)
            variant = jnp.where(r == 0, 0, jnp.where(r == rows - NA_QROWS, 2, 1))
            qs.append(pl.multiple_of(blk * nq, nq))
            ks.append(pl.multiple_of(base * GRID_W, GRID_W))
            q = q_ref[0, 0, pl.ds(qs[i], nq), :]
            k = k_ref[0, 0, pl.ds(ks[i], nk), :]
            s_lat.append(lax.dot_general(q, k, (((1,), (1,)), ((), ())), preferred_element_type=F32)
                         + bias_ref[variant])
            s_ctx.append(lax.dot_general(q, kc, (((1,), (1,)), ((), ())), preferred_element_type=F32))
        m = [jnp.maximum(jnp.max(s_lat[i], axis=1, keepdims=True), jnp.max(s_ctx[i], axis=1, keepdims=True))
             for i in blocks]
        e_lat = [jnp.exp(s_lat[i] - m[i]) for i in blocks]
        e_ctx = [jnp.exp(s_ctx[i] - m[i]) for i in blocks]
        p_lat = [e.astype(BF16) for e in e_lat]
        p_ctx = [e.astype(BF16) for e in e_ctx]
        rl = [1.0 / (jnp.sum(e_lat[i], axis=1, keepdims=True) + jnp.sum(e_ctx[i], axis=1, keepdims=True))
              for i in blocks]
        o = [jnp.dot(p_lat[i], v_ref[0, 0, pl.ds(ks[i], nk), :], preferred_element_type=F32)
             + jnp.dot(p_ctx[i], vc, preferred_element_type=F32) for i in blocks]
        for i in blocks:
            o_ref[0, 0, pl.ds(qs[i], nq), :] = (o[i] * rl[i] * z_ref[0, 0, pl.ds(qs[i], nq), :]).astype(BF16)
        return carry

    lax.fori_loop(0, rows_per_step // (NA_QROWS * nb), body, 0)


def _na_call(qk, v, qk_ctx, v_ctx, z, toep, l):
    b_, _, t_, dh = v.shape
    tc = v_ctx.shape[2]
    rows = t_ // GRID_W
    rows_per_step = min(rows, 64)
    tq = rows_per_step * GRID_W
    nvar, nq, nk = 3, NA_QROWS * GRID_W, NA_KROWS * GRID_W
    n_dr = toep.shape[2]
    blocks = (2 * _nbytes((tq, dh), BF16) + 2 * _nbytes((t_, dh), BF16) + 2 * _nbytes((tc, dh), BF16)
              + _nbytes((tq, dh), F32) + _nbytes((n_dr, GRID_W, LANES), F32))
    return pl.pallas_call(
        functools.partial(_na_kernel, rows=rows, rows_per_step=rows_per_step),
        grid=(b_, B_HEADS, t_ // tq),
        in_specs=[pl.BlockSpec((1, 1, tq, dh), lambda b, h, i: (b, h, i, 0)),
                  pl.BlockSpec((1, 1, t_, dh), lambda b, h, i: (b, B_HEADS + h, 0, 0)),
                  pl.BlockSpec((1, 1, t_, dh), lambda b, h, i: (b, h, 0, 0)),
                  pl.BlockSpec((1, 1, tc, dh), lambda b, h, i: (b, B_HEADS + h, 0, 0)),
                  pl.BlockSpec((1, 1, tc, dh), lambda b, h, i: (b, h, 0, 0)),
                  pl.BlockSpec((1, 1, tq, dh), lambda b, h, i: (b, h, i, 0)),
                  pl.BlockSpec((None, None, n_dr, GRID_W, GRID_W), lambda b, h, i: (l, h, 0, 0, 0))],
        out_specs=pl.BlockSpec((1, 1, tq, dh), lambda b, h, i: (b, h, i, 0)),
        out_shape=jax.ShapeDtypeStruct((b_, B_HEADS, t_, dh), BF16),
        scratch_shapes=[pltpu.VMEM((nvar, nq, nk), F32)],
        compiler_params=_params(3, _vmem_limit(blocks, _nbytes((nvar, nq, nk), F32)
                                               + 8 * NA_UNROLL * _nbytes((nq, nk + tc), F32))),
        name="na_attention",
    )(qk, qk, v, qk_ctx, v_ctx, z, toep)


def _ctx_attn_kernel(q_ref, k_ref, v_ref, z_ref, o_ref):
    q = q_ref[0, 0]
    s = lax.dot_general(q, k_ref[0, 0], (((1,), (1,)), ((), ())), preferred_element_type=F32)
    m = jnp.max(s, axis=1, keepdims=True)
    p = jnp.exp(s - m)
    rl = 1.0 / jnp.sum(p, axis=1, keepdims=True)
    o = jnp.dot(p.astype(BF16), v_ref[0, 0], preferred_element_type=F32)
    o_ref[0, 0] = (o * rl * z_ref[0, 0]).astype(BF16)


def _ctx_attn_call(qk_ctx, v_ctx, z_ctx):
    b_, _, tc, dh = v_ctx.shape
    blk = (1, 1, tc, dh)
    return pl.pallas_call(
        _ctx_attn_kernel,
        grid=(b_, B_HEADS),
        in_specs=[pl.BlockSpec(blk, lambda b, h: (b, h, 0, 0)),
                  pl.BlockSpec(blk, lambda b, h: (b, B_HEADS + h, 0, 0)),
                  pl.BlockSpec(blk, lambda b, h: (b, h, 0, 0)),
                  pl.BlockSpec(blk, lambda b, h: (b, h, 0, 0))],
        out_specs=pl.BlockSpec(blk, lambda b, h: (b, h, 0, 0)),
        out_shape=jax.ShapeDtypeStruct((b_, B_HEADS, tc, dh), BF16),
        compiler_params=_params(2, _vmem_limit(5 * _nbytes((tc, dh), F32), 4 * _nbytes((tc, tc), F32))),
        name="ctx_attention",
    )(qk_ctx, qk_ctx, v_ctx, z_ctx)


def _merge1_kernel(ya_ref, yb_ref, wa_ref, wb_ref, sa_ref, sb_ref, o_ref, ybcat_ref):
    @pl.when(pl.program_id(2) == 0)
    def _():
        for h in range(B_HEADS):
            ybcat_ref[:, h * B_DH:(h + 1) * B_DH] = yb_ref[0, h]

    ua = jnp.dot(ya_ref[0], wa_ref[...], preferred_element_type=F32)
    ub = jnp.dot(ybcat_ref[...], wb_ref[...], preferred_element_type=F32)
    o_ref[0] = (sa_ref[0] * ua + sb_ref[0] * ub).astype(BF16)


def _merge1_call(ya, yb, wa, wb, sgab, l, *, tm, tn):
    b_, t_, d = ya.shape
    nj = d // tn
    blocks = (2 * _nbytes((tm, d), BF16) + 2 * _nbytes((d, tn), BF16) + 2 * _nbytes((tm, tn), F32)
              + _nbytes((tm, tn), BF16))
    return pl.pallas_call(
        _merge1_kernel,
        grid=(b_, t_ // tm, nj),
        in_specs=[pl.BlockSpec((1, tm, d), lambda b, i, j: (b, i, 0)),
                  pl.BlockSpec((1, B_HEADS, tm, B_DH), lambda b, i, j: (b, 0, i, 0)),
                  pl.BlockSpec((None, d, tn), lambda b, i, j: (l, 0, j)),
                  pl.BlockSpec((None, d, tn), lambda b, i, j: (l, 0, j)),
                  pl.BlockSpec((1, tm, tn), lambda b, i, j: (b, i, j)),
                  pl.BlockSpec((1, tm, tn), lambda b, i, j: (b, i, nj + j))],
        out_specs=pl.BlockSpec((1, tm, tn), lambda b, i, j: (b, i, j)),
        out_shape=jax.ShapeDtypeStruct((b_, t_, d), BF16),
        scratch_shapes=[pltpu.VMEM((tm, d), BF16)],
        compiler_params=_params(3, _vmem_limit(blocks, _nbytes((tm, d), BF16) + 3 * _nbytes((tm, tn), F32))),
        name="merge_branches",
    )(ya, yb, wa, wb, sgab, sgab)


def _merge2_kernel(u_ref, w_ref, b_ref, x_ref, g_ref, o_ref):
    y = jnp.dot(u_ref[0], w_ref[...], preferred_element_type=F32) + b_ref[...]
    o_ref[0] = x_ref[0] + g_ref[0] * y


def _merge2_call(u, wo, bo, x, gate, l, *, tm, tn):
    b_, t_, d = u.shape
    blocks = _nbytes((tm, d), BF16) + _nbytes((d, tn), BF16) + 2 * _nbytes((tm, tn), F32)
    return pl.pallas_call(
        _merge2_kernel,
        grid=(b_, t_ // tm, d // tn),
        in_specs=[pl.BlockSpec((1, tm, d), lambda b, i, j: (b, i, 0)),
                  pl.BlockSpec((None, d, tn), lambda b, i, j: (l, 0, j)),
                  pl.BlockSpec((None, 1, tn), lambda b, i, j: (l, 0, j)),
                  pl.BlockSpec((1, tm, tn), lambda b, i, j: (b, i, j)),
                  pl.BlockSpec((1, 1, tn), lambda b, i, j: (b, 0, j))],
        out_specs=pl.BlockSpec((1, tm, tn), lambda b, i, j: (b, i, j)),
        out_shape=jax.ShapeDtypeStruct((b_, t_, d), F32),
        compiler_params=_params(3, _vmem_limit(blocks, 2 * _nbytes((tm, tn), F32))),
        name="merge_out",
    )(u, wo, bo, x, gate)


def _in_offsets(d_model):
    a_width = A_HEADS * A_DV
    b_width = B_HEADS * B_DH
    sizes = (A_HEADS * A_DK, A_HEADS * A_DK, a_width, a_width, a_width, A_HEADS, A_HEADS, A_HEADS, A_HEADS,
             b_width, b_width, b_width, b_width, d_model, d_model)
    offs = np.concatenate([[0], np.cumsum(sizes)])
    return {n: (int(offs[i]), int(sizes[i])) for i, n in enumerate(IN_NAMES)}


WIDE_GROUPS = (('aq', 'ak', 'av', 'ao', 'az'), ('bq', 'bk', 'bv', 'bz', 'ga', 'gb'))
GATE_ORDER = (('ai_f', 'ai_b'), ('af_f', 'af_b'))


def _prep_in_weights(w, b, na_q_g, na_k_g):
    offs = _in_offsets(w.shape[1])

    def seg(x, name):
        o, s = offs[name]
        return x[..., o:o + s]

    def gate_tiles(x):
        parts = []
        for names in GATE_ORDER:
            g = jnp.concatenate([seg(x, n) for n in names], axis=-1)
            parts.append(jnp.pad(g, [(0, 0)] * (g.ndim - 1) + [(0, LANES - g.shape[-1])]))
        return jnp.concatenate(parts, axis=-1)

    groups, col = [], {}
    for gi, names in enumerate(WIDE_GROUPS):
        start = offs[names[0]][0]
        stop = offs[names[-1]][0] + offs[names[-1]][1]
        for n in names:
            assert (offs[n][0] - start) % LANES == 0
            col[n] = (gi, offs[n][0] - start)
        groups.append((w[..., start:stop].astype(BF16), b[:, None, start:stop].astype(F32)))
    col['gates'] = (len(groups), 0)
    groups.append((gate_tiles(w).astype(BF16), gate_tiles(b)[:, None, :].astype(F32)))
    scale = B_DH ** -0.5
    gain = jnp.concatenate([jnp.tile(na_q_g * scale, (1, B_HEADS)),
                            jnp.tile(na_k_g, (1, B_HEADS))], axis=-1)[:, None, :].astype(F32)
    return groups, col, gain


def _rope_tables(n_tok, with_rope):
    nf = A_DK // 4
    kscale = A_DK ** -0.5
    if with_rope:
        t = jnp.arange(n_tok)
        rowp = (t // GRID_W).astype(F32)
        colp = (t % GRID_W).astype(F32)
        inv = ROPE_BASE ** (-jnp.arange(nf, dtype=F32) / nf)
        ar = rowp[:, None] * inv
        ac = colp[:, None] * inv
        cos = jnp.concatenate([jnp.cos(ar), jnp.cos(ar), jnp.cos(ac), jnp.cos(ac)], axis=1)
        sin = jnp.concatenate([-jnp.sin(ar), jnp.sin(ar), -jnp.sin(ac), jnp.sin(ac)], axis=1)
    else:
        cos = jnp.ones((n_tok, A_DK), F32)
        sin = jnp.zeros((n_tok, A_DK), F32)
    tq = jnp.stack([cos, sin])
    return jnp.stack([tq, tq * kscale])


def _na_column_tables(rpb):
    n_dc = 2 * NA_KW - 1
    qc = np.arange(GRID_W)
    kc = np.arange(GRID_W)
    ws = np.clip(qc - NA_KW // 2, 0, GRID_W - NA_KW)
    cvalid = (kc[None, :] >= ws[:, None]) & (kc[None, :] < ws[:, None] + NA_KW)
    dc = np.clip(kc[None, :] - qc[:, None] + NA_KW - 1, 0, n_dc - 1)
    csel = np.eye(n_dc, dtype=np.float32)[dc.reshape(-1)]
    t = jnp.einsum('lhuv,cv->lhuc', rpb.astype(F32), jnp.asarray(csel), precision=lax.Precision.HIGHEST)
    t = t.reshape(t.shape[:3] + (GRID_W, GRID_W))
    return jnp.where(jnp.asarray(cvalid), t, NEG_BIG)


def _project(h, groups, col, gain, l, rope_tab, *, tm):
    b_, t_, d = h.shape
    tn = 1024
    wq = A_HEADS * A_DK
    wv = A_HEADS * A_DV
    wh = B_HEADS * B_DH
    row_spec = pl.BlockSpec((1, tm, tn), lambda b, i, j: (b, i, j))
    hm_spec = pl.BlockSpec((1, tn // B_DH, tm, B_DH), lambda b, i, j: (b, j, i, 0))

    def call(segs, width, *args, **kw):
        names = (segs,) if isinstance(segs, str) else segs
        assert len({col[n][0] for n in names}) == 1
        ranges = tuple((col[n][1], width) for n in names)
        return _proj_call(h, groups[col[names[0]][0]], l, ranges[0] if len(names) == 1 else ranges, *args, **kw)

    res = {}
    res['qk'] = call(
        'aq', 2 * wq, _ep_rope, [rope_tab],
        [pl.BlockSpec((None, 2, tm, A_DK), lambda b, i, j: (j, 0, i, 0))],
        jax.ShapeDtypeStruct((b_, t_, 2 * wq), BF16), row_spec, tm=tm, tn=tn, name="proj_qk_rope")
    res['av'] = call(
        'av', wv, _ep_plain, [], [],
        jax.ShapeDtypeStruct((b_, t_, wv), F32), row_spec, tm=tm, tn=tn, name="proj_av")
    res['og'] = call(
        ('ao', 'az'), wv, _ep_outgate, [], [],
        jax.ShapeDtypeStruct((b_, t_, wv), F32),
        pl.BlockSpec((1, tm, tn // 2), lambda b, i, j: (b, i, j)), tm=tm, tn=tn // 2, name="proj_outgate")
    res['gates'] = call(
        'gates', 2 * LANES, _ep_plain, [], [],
        jax.ShapeDtypeStruct((b_, t_, 2 * LANES), F32),
        pl.BlockSpec((1, tm, 2 * LANES), lambda b, i, j: (b, i, j)), tm=tm, tn=2 * LANES, name="proj_gates")
    res['bqk'] = call(
        'bq', 2 * wh, _ep_headnorm, [gain],
        [pl.BlockSpec((None, 1, tn), lambda b, i, j: (l, 0, j))],
        jax.ShapeDtypeStruct((b_, 2 * B_HEADS, t_, B_DH), BF16), hm_spec, tm=tm, tn=tn, name="proj_bqk_norm")
    res['bv'] = call(
        'bv', wh, _ep_headmajor, [], [],
        jax.ShapeDtypeStruct((b_, B_HEADS, t_, B_DH), BF16), hm_spec, tm=tm, tn=tn, name="proj_bv")
    res['bz'] = call(
        'bz', wh, _ep_headmajor_silu, [], [],
        jax.ShapeDtypeStruct((b_, B_HEADS, t_, B_DH), F32), hm_spec, tm=tm, tn=tn, name="proj_bz_silu")
    res['gab'] = call(
        'ga', 2 * d, _ep_sigmoid, [], [],
        jax.ShapeDtypeStruct((b_, t_, 2 * d), F32), row_spec, tm=tm, tn=tn, name="proj_gab_sigmoid")
    return res


def kernel(x, c, ctx, c_ctx, w_mod, b_mod, norm_g, w_in, b_in, a_norm_g, w_br_a, w_br_b, na_q_g, na_k_g, na_rpb, w_out, b_out):
    bsz, n_tok, d = x.shape
    n_ctx = ctx.shape[1]
    depth = w_mod.shape[0]
    assert n_tok % MLSTM_CHUNK == 0 and n_ctx % MLSTM_CHUNK == 0 and n_tok % (NA_QROWS * GRID_W) == 0
    mod_rows = 16
    assert bsz + 1 <= mod_rows
    tm_lat = min(1024, n_tok)
    tm_ctx = min(1024, n_ctx)
    te_lat = min(512, n_tok)
    te_ctx = min(512, n_ctx)

    rope_lat = _rope_tables(n_tok, True)
    rope_ctx = _rope_tables(n_ctx, False)
    w_in_b, col, qk_gain = _prep_in_weights(w_in, b_in, na_q_g, na_k_g)
    bias = _na_column_tables(na_rpb)
    w_mod_b = w_mod.astype(BF16)
    b_mod_r = b_mod[:, None, :]
    wa = w_br_a.astype(BF16)
    wb = w_br_b.astype(BF16)
    wo = w_out.astype(BF16)
    bo = b_out[:, None, :]
    a_g = a_norm_g[:, None, :]
    c_rows = jnp.zeros((mod_rows, d), F32).at[:bsz].set(c).at[bsz].set(c_ctx)
    zero_state = (jnp.zeros((bsz, 2 * A_HEADS, A_DK, A_DV), F32),
                  jnp.zeros((bsz, 2 * A_HEADS, 1, A_DK), F32),
                  jnp.zeros((bsz, 1, LANES), F32))

    def merge(ya, yb, sgab, x_in, gate, l, tm):
        u = _merge1_call(ya, yb, wa, wb, sgab, l, tm=tm, tn=512)
        return _merge2_call(u, wo, bo, x_in, gate, l, tm=tm, tn=1024)

    for l in range(depth):
        need_ctx = l < depth - 1
        mod = _mod_call(c_rows, w_mod_b, b_mod_r, l)
        shift, scale, gate = (mod[:bsz, k * d:(k + 1) * d].reshape(bsz, 1, d) for k in range(3))
        shift_c, scale_c, gate_c = (jnp.broadcast_to(mod[bsz, k * d:(k + 1) * d].reshape(1, 1, d), (bsz, 1, d))
                                    for k in range(3))
        g_row = norm_g[l].reshape(1, d)
        h_lat = _modulate_call(x, g_row, shift, scale, te_lat)
        h_ctx = _modulate_call(ctx, g_row, shift_c, scale_c, te_ctx)
        pl_ = _project(h_lat, w_in_b, col, qk_gain, l, rope_lat, tm=tm_lat)
        pc_ = _project(h_ctx, w_in_b, col, qk_gain, l, rope_ctx, tm=tm_ctx)

        hcf, hcb, c1, n1, m1 = _mlstm_call(pc_['qk'], pc_['av'], pc_['gates'], zero_state,
                                           chunk=MLSTM_CHUNK, dirs=(0, 1), name="mlstm_ctx")
        hlf = _mlstm_call(pl_['qk'], pl_['av'], pl_['gates'], (c1, n1, m1),
                          chunk=MLSTM_CHUNK, dirs=(0,), name="mlstm_lat_fwd")[0]
        ya_l = _mlstm_call(pl_['qk'], pl_['av'], pl_['gates'], (c1, n1, m1), chunk=MLSTM_CHUNK, dirs=(1,),
                           name="mlstm_lat_bwd", fuse=(hlf, pl_['og'], a_g, l))[0]

        yb_l = _na_call(pl_['bqk'], pl_['bv'], pc_['bqk'], pc_['bv'], pl_['bz'], bias, l)

        x = merge(ya_l, yb_l, pl_['gab'], x, gate, l, tm_lat)
        if need_ctx:
            ya_c = _mlstm_out_call(hcf, hcb, pc_['og'], a_g, l, te_ctx)
            yb_c = _ctx_attn_call(pc_['bqk'], pc_['bv'], pc_['bz'])
            ctx = merge(ya_c, yb_c, pc_['gab'], ctx, gate_c, l, tm_ctx)
    return x
```

```python
import functools

import numpy as np
import jax
import jax.numpy as jnp
from jax import lax
from jax.experimental import pallas as pl
from jax.experimental.pallas import tpu as pltpu

F32 = jnp.float32
BF16 = jnp.bfloat16

GRID_W = 64
A_HEADS = 8
A_DK = 128
A_DV = 256
B_HEADS = 16
B_DH = 128
NA_KH = 8
NA_KW = 16
ROPE_BASE = 10000.0
EPS = 1e-6
IN_NAMES = ('aq', 'ak', 'av', 'ao', 'az', 'ai_f', 'af_f', 'ai_b', 'af_b', 'bq', 'bk', 'bv', 'bz', 'ga', 'gb')

LANES = 128
SUBLANES = 8
VMEM_LIMIT_CAP = 56 * 1024 * 1024

MLSTM_CHUNK = 256
NA_QROWS = 4
NA_KROWS = NA_KH + NA_QROWS - 1
NA_UNROLL = 4
NEG_BIG = -1e30


def _vmem_limit(block_bytes, temp_bytes=0):
    need = 2 * block_bytes + temp_bytes + (4 << 20)
    return int(min(max(need, 16 << 20), VMEM_LIMIT_CAP))


def _nbytes(shape, dtype):
    return int(np.prod(shape)) * jnp.dtype(dtype).itemsize


def _params(n_grid, vmem):
    return pltpu.CompilerParams(dimension_semantics=("arbitrary",) * n_grid, vmem_limit_bytes=vmem)


def _sigmoid(x):
    return 1.0 / (1.0 + jnp.exp(-x))


def _mod_kernel(c_ref, w_ref, b_ref, o_ref):
    c = c_ref[...]
    o_ref[...] = jnp.dot((c * _sigmoid(c)).astype(BF16), w_ref[...], preferred_element_type=F32) + b_ref[...]


def _mod_call(c_rows, w, b, l):
    m, d = c_rows.shape
    n = w.shape[2]
    tn = 1536
    return pl.pallas_call(
        _mod_kernel,
        grid=(n // tn,),
        in_specs=[pl.BlockSpec((m, d), lambda j: (0, 0)),
                  pl.BlockSpec((None, d, tn), lambda j: (l, 0, j)),
                  pl.BlockSpec((None, 1, tn), lambda j: (l, 0, j))],
        out_specs=pl.BlockSpec((m, tn), lambda j: (0, j)),
        out_shape=jax.ShapeDtypeStruct((m, n), F32),
        compiler_params=_params(1, _vmem_limit(_nbytes((d, tn), BF16))),
        name="mod_dense",
    )(c_rows, w, b)


def _modulate_kernel(x_ref, g_ref, sh_ref, sc_ref, o_ref):
    x = x_ref[0]
    y = x * lax.rsqrt(jnp.mean(x * x, axis=-1, keepdims=True) + EPS) * g_ref[...]
    o_ref[0] = (y * (1.0 + sc_ref[0]) + sh_ref[0]).astype(BF16)


def _modulate_call(x, g, shift, scale, tm):
    b_, t_, d = x.shape
    return pl.pallas_call(
        _modulate_kernel,
        grid=(b_, t_ // tm),
        in_specs=[pl.BlockSpec((1, tm, d), lambda b, i: (b, i, 0)),
                  pl.BlockSpec((1, d), lambda b, i: (0, 0)),
                  pl.BlockSpec((1, 1, d), lambda b, i: (b, 0, 0)),
                  pl.BlockSpec((1, 1, d), lambda b, i: (b, 0, 0))],
        out_specs=pl.BlockSpec((1, tm, d), lambda b, i: (b, i, 0)),
        out_shape=jax.ShapeDtypeStruct((b_, t_, d), BF16),
        compiler_params=_params(2, _vmem_limit(_nbytes((tm, d), F32) + _nbytes((tm, d), BF16),
                                               2 * _nbytes((tm, d), F32))),
        name="modulate",
    )(x, g, shift, scale)


def _proj_kernel(a_ref, w_ref, b_ref, *rest, epilogue):
    acc = jnp.dot(a_ref[0], w_ref[...], preferred_element_type=F32) + b_ref[...]
    epilogue(acc, *rest)


def _proj_pair_kernel(a_ref, w0_ref, b0_ref, w1_ref, b1_ref, *rest, epilogue):
    a = a_ref[0]
    acc0 = jnp.dot(a, w0_ref[...], preferred_element_type=F32) + b0_ref[...]
    acc1 = jnp.dot(a, w1_ref[...], preferred_element_type=F32) + b1_ref[...]
    epilogue(acc0, acc1, *rest)


def _proj_call(a, wb, l, cols, epilogue, extras, extra_specs, out_shape, out_spec, *, tm, tn, name):
    w, bias = wb
    b_, t_, k = a.shape
    ranges = cols if isinstance(cols[0], tuple) else (cols,)
    n = ranges[0][1]
    w_specs, w_args = [], []
    for c0, cn in ranges:
        assert cn == n and c0 % tn == 0 and n % tn == 0
        jb = c0 // tn
        w_specs += [pl.BlockSpec((None, k, tn), lambda b, i, j, jb=jb: (l, 0, jb + j)),
                    pl.BlockSpec((None, 1, tn), lambda b, i, j, jb=jb: (l, 0, jb + j))]
        w_args += [w, bias]
    blocks = _nbytes((tm, k), BF16) + len(ranges) * _nbytes((k, tn), BF16) + 2 * _nbytes((tm, tn), F32)
    body = _proj_kernel if len(ranges) == 1 else _proj_pair_kernel
    return pl.pallas_call(
        functools.partial(body, epilogue=epilogue),
        grid=(b_, t_ // tm, n // tn),
        in_specs=[pl.BlockSpec((1, tm, k), lambda b, i, j: (b, i, 0))] + w_specs + list(extra_specs),
        out_specs=out_spec,
        out_shape=out_shape,
        compiler_params=_params(3, _vmem_limit(blocks, (2 + len(ranges)) * _nbytes((tm, tn), F32))),
        name=name,
    )(a, *w_args, *extras)


def _ep_plain(acc, o_ref):
    o_ref[0] = acc.astype(o_ref.dtype)


def _ep_sigmoid(acc, o_ref):
    o_ref[0] = _sigmoid(acc)


def _ep_rope(acc, tab_ref, o_ref):
    cos = tab_ref[0]
    sin = tab_ref[1]
    quarter = A_DK // 4
    lane = lax.broadcasted_iota(jnp.int32, cos.shape, 1)
    first_half = (lane % (2 * quarter)) < quarter
    for h in range(acc.shape[1] // A_DK):
        x = acc[:, h * A_DK:(h + 1) * A_DK]
        partner = jnp.where(first_half, pltpu.roll(x, A_DK - quarter, 1), pltpu.roll(x, quarter, 1))
        o_ref[0, :, h * A_DK:(h + 1) * A_DK] = (x * cos + partner * sin).astype(o_ref.dtype)


def _ep_outgate(acc_o, acc_z, o_ref):
    o_ref[0] = _sigmoid(acc_o) * (acc_z * _sigmoid(acc_z))


def _ep_headnorm(acc, gain_ref, o_ref):
    for h in range(acc.shape[1] // B_DH):
        x = acc[:, h * B_DH:(h + 1) * B_DH]
        y = x * lax.rsqrt(jnp.mean(x * x, axis=-1, keepdims=True) + EPS)
        o_ref[0, h] = (y * gain_ref[:, h * B_DH:(h + 1) * B_DH]).astype(o_ref.dtype)


def _ep_headmajor(acc, o_ref):
    for h in range(acc.shape[1] // B_DH):
        o_ref[0, h] = acc[:, h * B_DH:(h + 1) * B_DH].astype(o_ref.dtype)


def _ep_headmajor_silu(acc, o_ref):
    for h in range(acc.shape[1] // B_DH):
        x = acc[:, h * B_DH:(h + 1) * B_DH]
        o_ref[0, h] = (x * _sigmoid(x)).astype(o_ref.dtype)


def _running_max_rows(x, reverse):
    n = x.shape[0]
    rowi = lax.broadcasted_iota(jnp.int32, x.shape, 0)
    s = 1
    while s < n:
        if s < SUBLANES:
            if reverse:
                sh = jnp.where(rowi < n - s, pltpu.roll(x, n - s, 0), -jnp.inf)
            else:
                sh = jnp.where(rowi >= s, pltpu.roll(x, s, 0), -jnp.inf)
        else:
            pad = jnp.full((s, x.shape[1]), -jnp.inf, x.dtype)
            sh = jnp.concatenate([x[s:], pad], 0) if reverse else jnp.concatenate([pad, x[:n - s]], 0)
        x = jnp.maximum(x, sh)
        s *= 2
    return x


def _mlstm_kernel(*refs, chunk, dirs, fuse_out):
    L = chunk
    nd = len(dirs)
    in_refs = [refs[4 * i:4 * i + 4] for i in range(nd)]
    c0_ref, n0_ref, m0_ref = refs[4 * nd:4 * nd + 3]
    pos = 4 * nd + 3
    if fuse_out:
        hprev_ref, og_ref, gain_ref = refs[pos:pos + 3]
        pos += 3
    out_refs = refs[pos:pos + nd]
    c_ref, n_ref, m_ref = refs[pos + nd:pos + nd + 3]

    @pl.when(pl.program_id(1) == 0)
    def _():
        c_ref[...] = c0_ref[...]
        n_ref[...] = n0_ref[...]
        m_ref[...] = m0_ref[...]

    row = lax.broadcasted_iota(jnp.int32, (L, L), 0)
    col = lax.broadcasted_iota(jnp.int32, (L, L), 1)
    lane = lax.broadcasted_iota(jnp.int32, (1, LANES), 1)
    m_all = m_ref[0]
    m_next = m_all
    for d, (q_ref, k_ref, v_ref, g_ref), h_ref in zip(dirs, in_refs, out_refs):
        mask = (row >= col) if d == 0 else (row <= col)
        gates = g_ref[0]
        ig = gates[:, :LANES]
        fg = gates[:, LANES:]
        logf = -(jnp.maximum(-fg, 0.0) + jnp.log1p(jnp.exp(-jnp.abs(fg))))
        bcum = jnp.dot(mask.astype(F32), logf, precision=lax.Precision.HIGHEST, preferred_element_type=F32)
        a = ig - bcum
        m_run = jnp.maximum(_running_max_rows(a, reverse=(d == 1)), m_all)
        w_int_all = jnp.exp(m_all - m_run)
        e_mq_all = jnp.exp(-(bcum + m_run))
        last = L - 1 if d == 0 else 0
        m_last = m_run[last:last + 1]
        w_prev_all = jnp.exp(m_all - m_last)
        w_tok_all = jnp.exp(a - m_last)
        in_dir = (lane >= d * A_HEADS) & (lane < (d + 1) * A_HEADS)
        m_next = jnp.where(in_dir, bcum[last:last + 1] + m_last, m_next)
        a_t = a.T
        heads = range(A_HEADS)
        chs = [d * A_HEADS + h for h in heads]
        sls = [slice(h * A_DV, (h + 1) * A_DV) for h in heads]
        qs = [q_ref[0, :, h * A_DK:(h + 1) * A_DK] for h in heads]
        ks = [k_ref[0, :, h * A_DK:(h + 1) * A_DK] for h in heads]
        decay = [jnp.where(mask, jnp.exp(a_t[ch:ch + 1, :] - m_run[:, ch:ch + 1]), 0.0) for ch in chs]
        s_raw = [lax.dot_general(qs[h], ks[h], (((1,), (1,)), ((), ())), preferred_element_type=F32)
                 for h in heads]
        s_dec = [s_raw[h] * decay[h] for h in heads]
        s_bf = [s.astype(BF16) for s in s_dec]
        qn = [jnp.sum(qs[h].astype(F32) * n_ref[0, chs[h]], axis=1, keepdims=True) for h in heads]
        s_sum = [jnp.sum(s, axis=1, keepdims=True) for s in s_dec]
        rs, wrs = [], []
        for h in heads:
            ch = chs[h]
            w_int = w_int_all[:, ch:ch + 1]
            r = 1.0 / jnp.maximum(jnp.abs(s_sum[h] + w_int * qn[h]), e_mq_all[:, ch:ch + 1])
            rs.append(r)
            wrs.append(w_int * r)
        h_intra = [jnp.dot(s_bf[h], v_ref[0, :, sls[h]].astype(BF16), preferred_element_type=F32) for h in heads]
        h_inter = [jnp.dot(qs[h], c_ref[0, chs[h]].astype(BF16), preferred_element_type=F32) for h in heads]
        hv = [h_intra[h] * rs[h] + h_inter[h] * wrs[h] for h in heads]
        if fuse_out:
            xs = [hprev_ref[0, :, sls[h]] + hv[h] for h in heads]
            inv = [lax.rsqrt(jnp.mean(x * x, axis=-1, keepdims=True) + EPS) for x in xs]
            for h in heads:
                y = xs[h] * inv[h] * gain_ref[:, sls[h]]
                h_ref[0, :, sls[h]] = (y * og_ref[0, :, sls[h]]).astype(h_ref.dtype)
        else:
            for h in heads:
                h_ref[0, :, sls[h]] = hv[h]
        wv = [(w_tok_all[:, ch:ch + 1] * v_ref[0, :, sl]).astype(BF16) for ch, sl in zip(chs, sls)]
        c_upd = [lax.dot_general(ks[h], wv[h], (((0,), (0,)), ((), ())), preferred_element_type=F32)
                 for h in heads]
        for h in heads:
            ch = chs[h]
            w_prev = w_prev_all[:, ch:ch + 1]
            c_ref[0, ch] = w_prev * c_ref[0, ch] + c_upd[h]
            n_ref[0, ch] = (w_prev * n_ref[0, ch]
                            + jnp.sum(w_tok_all[:, ch:ch + 1] * ks[h].astype(F32), axis=0, keepdims=True))
    m_ref[0] = m_next


def _mlstm_call(qk, v, gates, state, *, chunk, dirs, name, fuse=None):
    b_, t_, _ = v.shape
    nc = t_ // chunk
    wq = A_HEADS * A_DK
    wv = A_HEADS * A_DV
    c0, n0, m0 = state

    def spec(width, blk, d):
        which = (lambda c: c) if d == 0 else (lambda c: nc - 1 - c)
        return pl.BlockSpec((1, chunk, width), lambda b, c: (b, which(c), blk))

    st_specs = [pl.BlockSpec((1, 2 * A_HEADS, A_DK, A_DV), lambda b, c: (b, 0, 0, 0)),
                pl.BlockSpec((1, 2 * A_HEADS, 1, A_DK), lambda b, c: (b, 0, 0, 0)),
                pl.BlockSpec((1, 1, LANES), lambda b, c: (b, 0, 0))]
    in_specs, args = [], []
    for d in dirs:
        in_specs += [spec(wq, 0, d), spec(wq, 1, d), spec(wv, 0, d), spec(2 * LANES, 0, d)]
        args += [qk, qk, v, gates]
    in_specs += st_specs
    args += [c0, n0, m0]
    out_dtype = F32
    nblk = len(dirs)
    if fuse is not None:
        assert len(dirs) == 1
        h_prev, og, gain, l = fuse
        in_specs += [spec(wv, 0, dirs[0]), spec(wv, 0, dirs[0]),
                     pl.BlockSpec((None, 1, wv), lambda b, c: (l, 0, 0))]
        args += [h_prev, og, gain]
        out_dtype = BF16
        nblk += 2
    blocks = (len(dirs) * (2 * _nbytes((chunk, wq), BF16) + _nbytes((chunk, 2 * LANES), F32))
              + (len(dirs) + nblk) * _nbytes((chunk, wv), F32) + 2 * _nbytes((2 * A_HEADS, A_DK, A_DV), F32))
    return pl.pallas_call(
        functools.partial(_mlstm_kernel, chunk=chunk, dirs=tuple(dirs), fuse_out=fuse is not None),
        grid=(b_, nc),
        in_specs=in_specs,
        out_specs=[spec(wv, 0, d) for d in dirs] + st_specs,
        out_shape=[jax.ShapeDtypeStruct((b_, t_, wv), out_dtype) for _ in dirs]
                  + [jax.ShapeDtypeStruct(c0.shape, F32), jax.ShapeDtypeStruct(n0.shape, F32),
                     jax.ShapeDtypeStruct(m0.shape, F32)],
        compiler_params=_params(2, _vmem_limit(blocks, 6 * A_HEADS * _nbytes((chunk, chunk), F32))),
        name=name,
    )(*args)


def _mlstm_out_kernel(hf_ref, hb_ref, og_ref, g_ref, o_ref):
    for h in range(A_HEADS):
        sl = slice(h * A_DV, (h + 1) * A_DV)
        x = hf_ref[0, :, sl] + hb_ref[0, :, sl]
        y = x * lax.rsqrt(jnp.mean(x * x, axis=-1, keepdims=True) + EPS) * g_ref[:, sl]
        o_ref[0, :, sl] = (y * og_ref[0, :, sl]).astype(BF16)


def _mlstm_out_call(hf, hb, og, g, l, tm):
    b_, t_, w = hf.shape
    spec = pl.BlockSpec((1, tm, w), lambda b, i: (b, i, 0))
    return pl.pallas_call(
        _mlstm_out_kernel,
        grid=(b_, t_ // tm),
        in_specs=[spec, spec, spec, pl.BlockSpec((None, 1, w), lambda b, i: (l, 0, 0))],
        out_specs=spec,
        out_shape=jax.ShapeDtypeStruct((b_, t_, w), BF16),
        compiler_params=_params(2, _vmem_limit(4 * _nbytes((tm, w), F32), 2 * _nbytes((tm, w), F32))),
        name="mlstm_out",
    )(hf, hb, og, g)


def _na_row_variants(rows):
    assert NA_QROWS == NA_KH // 2 and rows % NA_QROWS == 0 and rows >= 2 * NA_KH, (
        "one clamped query block at the top and one at the bottom needs 4-row blocks and >= 16 grid rows")
    out = []
    for r in (0, NA_QROWS, rows - NA_QROWS):
        base = min(max(r - NA_KH // 2, 0), rows - NA_KROWS)
        table = []
        for a in range(NA_QROWS):
            qr = r + a
            rs = min(max(qr - NA_KH // 2, 0), rows - NA_KH)
            table.append([(base + i) - qr + NA_KH - 1 if rs <= base + i < rs + NA_KH else None
                          for i in range(NA_KROWS)])
        out.append(table)
    return out


def _na_kernel(q_ref, k_ref, v_ref, kc_ref, vc_ref, z_ref, toep_ref, o_ref, bias_ref, *, rows, rows_per_step):
    nq = NA_QROWS * GRID_W
    nk = NA_KROWS * GRID_W
    r0 = pl.program_id(2) * rows_per_step
    kc = kc_ref[0, 0]
    vc = vc_ref[0, 0]

    @pl.when(pl.program_id(2) == 0)
    def _():
        masked = jnp.full((GRID_W, GRID_W), NEG_BIG, F32)
        for x, table in enumerate(_na_row_variants(rows)):
            for a in range(NA_QROWS):
                for i in range(NA_KROWS):
                    dr = table[a][i]
                    bias_ref[x, a * GRID_W:(a + 1) * GRID_W, i * GRID_W:(i + 1) * GRID_W] = (
                        masked if dr is None else toep_ref[dr])

    nb = min(NA_UNROLL, rows_per_step // NA_QROWS)

    def body(it, carry):
        blocks = range(nb)
        qs, ks, s_lat, s_ctx = [], [], [], []
        for i in blocks:
            blk = it * nb + i
            r = r0 + blk * NA_QROWS
            base = jnp.clip(r - NA_KH // 2, 0, rows - NA_KROWS)
            variant = jnp.where(r == 0, 0, jnp.where(r == rows - NA_QROWS, 2, 1))
            qs.append(pl.multiple_of(blk * nq, nq))
            ks.append(pl.multiple_of(base * GRID_W, GRID_W))
            q = q_ref[0, 0, pl.ds(qs[i], nq), :]
            k = k_ref[0, 0, pl.ds(ks[i], nk), :]
            s_lat.append(lax.dot_general(q, k, (((1,), (1,)), ((), ())), preferred_element_type=F32)
                         + bias_ref[variant])
            s_ctx.append(lax.dot_general(q, kc, (((1,), (1,)), ((), ())), preferred_element_type=F32))
        m = [jnp.maximum(jnp.max(s_lat[i], axis=1, keepdims=True), jnp.max(s_ctx[i], axis=1, keepdims=True))
             for i in blocks]
        e_lat = [jnp.exp(s_lat[i] - m[i]) for i in blocks]
        e_ctx = [jnp.exp(s_ctx[i] - m[i]) for i in blocks]
        p_lat = [e.astype(BF16) for e in e_lat]
        p_ctx = [e.astype(BF16) for e in e_ctx]
        rl = [1.0 / (jnp.sum(e_lat[i], axis=1, keepdims=True) + jnp.sum(e_ctx[i], axis=1, keepdims=True))
              for i in blocks]
        o = [jnp.dot(p_lat[i], v_ref[0, 0, pl.ds(ks[i], nk), :], preferred_element_type=F32)
             + jnp.dot(p_ctx[i], vc, preferred_element_type=F32) for i in blocks]
        for i in blocks:
            o_ref[0, 0, pl.ds(qs[i], nq), :] = (o[i] * rl[i] * z_ref[0, 0, pl.ds(qs[i], nq), :]).astype(BF16)
        return carry

    lax.fori_loop(0, rows_per_step // (NA_QROWS * nb), body, 0)


def _na_call(qk, v, qk_ctx, v_ctx, z, toep, l):
    b_, _, t_, dh = v.shape
    tc = v_ctx.shape[2]
    rows = t_ // GRID_W
    rows_per_step = min(rows, 64)
    tq = rows_per_step * GRID_W
    nvar, nq, nk = 3, NA_QROWS * GRID_W, NA_KROWS * GRID_W
    n_dr = toep.shape[2]
    blocks = (2 * _nbytes((tq, dh), BF16) + 2 * _nbytes((t_, dh), BF16) + 2 * _nbytes((tc, dh), BF16)
              + _nbytes((tq, dh), F32) + _nbytes((n_dr, GRID_W, LANES), F32))
    return pl.pallas_call(
        functools.partial(_na_kernel, rows=rows, rows_per_step=rows_per_step),
        grid=(b_, B_HEADS, t_ // tq),
        in_specs=[pl.BlockSpec((1, 1, tq, dh), lambda b, h, i: (b, h, i, 0)),
                  pl.BlockSpec((1, 1, t_, dh), lambda b, h, i: (b, B_HEADS + h, 0, 0)),
                  pl.BlockSpec((1, 1, t_, dh), lambda b, h, i: (b, h, 0, 0)),
                  pl.BlockSpec((1, 1, tc, dh), lambda b, h, i: (b, B_HEADS + h, 0, 0)),
                  pl.BlockSpec((1, 1, tc, dh), lambda b, h, i: (b, h, 0, 0)),
                  pl.BlockSpec((1, 1, tq, dh), lambda b, h, i: (b, h, i, 0)),
                  pl.BlockSpec((None, None, n_dr, GRID_W, GRID_W), lambda b, h, i: (l, h, 0, 0, 0))],
        out_specs=pl.BlockSpec((1, 1, tq, dh), lambda b, h, i: (b, h, i, 0)),
        out_shape=jax.ShapeDtypeStruct((b_, B_HEADS, t_, dh), BF16),
        scratch_shapes=[pltpu.VMEM((nvar, nq, nk), F32)],
        compiler_params=_params(3, _vmem_limit(blocks, _nbytes((nvar, nq, nk), F32)
                                               + 8 * NA_UNROLL * _nbytes((nq, nk + tc), F32))),
        name="na_attention",
    )(qk, qk, v, qk_ctx, v_ctx, z, toep)


def _ctx_attn_kernel(q_ref, k_ref, v_ref, z_ref, o_ref):
    q = q_ref[0, 0]
    s = lax.dot_general(q, k_ref[0, 0], (((1,), (1,)), ((), ())), preferred_element_type=F32)
    m = jnp.max(s, axis=1, keepdims=True)
    p = jnp.exp(s - m)
    rl = 1.0 / jnp.sum(p, axis=1, keepdims=True)
    o = jnp.dot(p.astype(BF16), v_ref[0, 0], preferred_element_type=F32)
    o_ref[0, 0] = (o * rl * z_ref[0, 0]).astype(BF16)


def _ctx_attn_call(qk_ctx, v_ctx, z_ctx):
    b_, _, tc, dh = v_ctx.shape
    blk = (1, 1, tc, dh)
    return pl.pallas_call(
        _ctx_attn_kernel,
        grid=(b_, B_HEADS),
        in_specs=[pl.BlockSpec(blk, lambda b, h: (b, h, 0, 0)),
                  pl.BlockSpec(blk, lambda b, h: (b, B_HEADS + h, 0, 0)),
                  pl.BlockSpec(blk, lambda b, h: (b, h, 0, 0)),
                  pl.BlockSpec(blk, lambda b, h: (b, h, 0, 0))],
        out_specs=pl.BlockSpec(blk, lambda b, h: (b, h, 0, 0)),
        out_shape=jax.ShapeDtypeStruct((b_, B_HEADS, tc, dh), BF16),
        compiler_params=_params(2, _vmem_limit(5 * _nbytes((tc, dh), F32), 4 * _nbytes((tc, tc), F32))),
        name="ctx_attention",
    )(qk_ctx, qk_ctx, v_ctx, z_ctx)


def _merge1_kernel(ya_ref, yb_ref, wa_ref, wb_ref, sa_ref, sb_ref, o_ref, ybcat_ref):
    @pl.when(pl.program_id(2) == 0)
    def _():
        for h in range(B_HEADS):
            ybcat_ref[:, h * B_DH:(h + 1) * B_DH] = yb_ref[0, h]

    ua = jnp.dot(ya_ref[0], wa_ref[...], preferred_element_type=F32)
    ub = jnp.dot(ybcat_ref[...], wb_ref[...], preferred_element_type=F32)
    o_ref[0] = (sa_ref[0] * ua + sb_ref[0] * ub).astype(BF16)


def _merge1_call(ya, yb, wa, wb, sgab, l, *, tm, tn):
    b_, t_, d = ya.shape
    nj = d // tn
    blocks = (2 * _nbytes((tm, d), BF16) + 2 * _nbytes((d, tn), BF16) + 2 * _nbytes((tm, tn), F32)
              + _nbytes((tm, tn), BF16))
    return pl.pallas_call(
        _merge1_kernel,
        grid=(b_, t_ // tm, nj),
        in_specs=[pl.BlockSpec((1, tm, d), lambda b, i, j: (b, i, 0)),
                  pl.BlockSpec((1, B_HEADS, tm, B_DH), lambda b, i, j: (b, 0, i, 0)),
                  pl.BlockSpec((None, d, tn), lambda b, i, j: (l, 0, j)),
                  pl.BlockSpec((None, d, tn), lambda b, i, j: (l, 0, j)),
                  pl.BlockSpec((1, tm, tn), lambda b, i, j: (b, i, j)),
                  pl.BlockSpec((1, tm, tn), lambda b, i, j: (b, i, nj + j))],
        out_specs=pl.BlockSpec((1, tm, tn), lambda b, i, j: (b, i, j)),
        out_shape=jax.ShapeDtypeStruct((b_, t_, d), BF16),
        scratch_shapes=[pltpu.VMEM((tm, d), BF16)],
        compiler_params=_params(3, _vmem_limit(blocks, _nbytes((tm, d), BF16) + 3 * _nbytes((tm, tn), F32))),
        name="merge_branches",
    )(ya, yb, wa, wb, sgab, sgab)


def _merge2_kernel(u_ref, w_ref, b_ref, x_ref, g_ref, o_ref):
    y = jnp.dot(u_ref[0], w_ref[...], preferred_element_type=F32) + b_ref[...]
    o_ref[0] = x_ref[0] + g_ref[0] * y


def _merge2_call(u, wo, bo, x, gate, l, *, tm, tn):
    b_, t_, d = u.shape
    blocks = _nbytes((tm, d), BF16) + _nbytes((d, tn), BF16) + 2 * _nbytes((tm, tn), F32)
    return pl.pallas_call(
        _merge2_kernel,
        grid=(b_, t_ // tm, d // tn),
        in_specs=[pl.BlockSpec((1, tm, d), lambda b, i, j: (b, i, 0)),
                  pl.BlockSpec((None, d, tn), lambda b, i, j: (l, 0, j)),
                  pl.BlockSpec((None, 1, tn), lambda b, i, j: (l, 0, j)),
                  pl.BlockSpec((1, tm, tn), lambda b, i, j: (b, i, j)),
                  pl.BlockSpec((1, 1, tn), lambda b, i, j: (b, 0, j))],
        out_specs=pl.BlockSpec((1, tm, tn), lambda b, i, j: (b, i, j)),
        out_shape=jax.ShapeDtypeStruct((b_, t_, d), F32),
        compiler_params=_params(3, _vmem_limit(blocks, 2 * _nbytes((tm, tn), F32))),
        name="merge_out",
    )(u, wo, bo, x, gate)


def _in_offsets(d_model):
    a_width = A_HEADS * A_DV
    b_width = B_HEADS * B_DH
    sizes = (A_HEADS * A_DK, A_HEADS * A_DK, a_width, a_width, a_width, A_HEADS, A_HEADS, A_HEADS, A_HEADS,
             b_width, b_width, b_width, b_width, d_model, d_model)
    offs = np.concatenate([[0], np.cumsum(sizes)])
    return {n: (int(offs[i]), int(sizes[i])) for i, n in enumerate(IN_NAMES)}


WIDE_GROUPS = (('aq', 'ak', 'av', 'ao', 'az'), ('bq', 'bk', 'bv', 'bz', 'ga', 'gb'))
GATE_ORDER = (('ai_f', 'ai_b'), ('af_f', 'af_b'))


def _prep_in_weights(w, b, na_q_g, na_k_g):
    offs = _in_offsets(w.shape[1])

    def seg(x, name):
        o, s = offs[name]
        return x[..., o:o + s]

    def gate_tiles(x):
        parts = []
        for names in GATE_ORDER:
            g = jnp.concatenate([seg(x, n) for n in names], axis=-1)
            parts.append(jnp.pad(g, [(0, 0)] * (g.ndim - 1) + [(0, LANES - g.shape[-1])]))
        return jnp.concatenate(parts, axis=-1)

    groups, col = [], {}
    w16 = w.astype(BF16)
    for gi, names in enumerate(WIDE_GROUPS):
        start = offs[names[0]][0]
        stop = offs[names[-1]][0] + offs[names[-1]][1]
        if start % LANES == 0:
            start = 0
            stop = w.shape[-1]
        for n in names:
            assert (offs[n][0] - start) % LANES == 0
            col[n] = (gi, offs[n][0] - start)
        groups.append((w16[..., start:stop], b[:, None, start:stop].astype(F32)))
    col['gates'] = (len(groups), 0)
    groups.append((gate_tiles(w).astype(BF16), gate_tiles(b)[:, None, :].astype(F32)))
    scale = B_DH ** -0.5
    gain = jnp.concatenate([jnp.tile(na_q_g * scale, (1, B_HEADS)),
                            jnp.tile(na_k_g, (1, B_HEADS))], axis=-1)[:, None, :].astype(F32)
    return groups, col, gain


def _rope_tables(n_tok, with_rope):
    nf = A_DK // 4
    kscale = A_DK ** -0.5
    if with_rope:
        t = jnp.arange(n_tok)
        rowp = (t // GRID_W).astype(F32)
        colp = (t % GRID_W).astype(F32)
        inv = ROPE_BASE ** (-jnp.arange(nf, dtype=F32) / nf)
        ar = rowp[:, None] * inv
        ac = colp[:, None] * inv
        cos = jnp.concatenate([jnp.cos(ar), jnp.cos(ar), jnp.cos(ac), jnp.cos(ac)], axis=1)
        sin = jnp.concatenate([-jnp.sin(ar), jnp.sin(ar), -jnp.sin(ac), jnp.sin(ac)], axis=1)
    else:
        cos = jnp.ones((n_tok, A_DK), F32)
        sin = jnp.zeros((n_tok, A_DK), F32)
    tq = jnp.stack([cos, sin])
    return jnp.stack([tq, tq * kscale])


def _na_column_tables(rpb):
    n_dc = 2 * NA_KW - 1
    qc = np.arange(GRID_W)
    kc = np.arange(GRID_W)
    ws = np.clip(qc - NA_KW // 2, 0, GRID_W - NA_KW)
    cvalid = (kc[None, :] >= ws[:, None]) & (kc[None, :] < ws[:, None] + NA_KW)
    dc = np.clip(kc[None, :] - qc[:, None] + NA_KW - 1, 0, n_dc - 1)
    csel = np.eye(n_dc, dtype=np.float32)[dc.reshape(-1)]
    t = jnp.einsum('lhuv,cv->lhuc', rpb.astype(F32), jnp.asarray(csel), precision=lax.Precision.HIGHEST)
    t = t.reshape(t.shape[:3] + (GRID_W, GRID_W))
    return jnp.where(jnp.asarray(cvalid), t, NEG_BIG)


def _project(h, groups, col, gain, l, rope_tab, *, tm):
    b_, t_, d = h.shape
    tn = 1024
    wq = A_HEADS * A_DK
    wv = A_HEADS * A_DV
    wh = B_HEADS * B_DH
    row_spec = pl.BlockSpec((1, tm, tn), lambda b, i, j: (b, i, j))
    hm_spec = pl.BlockSpec((1, tn // B_DH, tm, B_DH), lambda b, i, j: (b, j, i, 0))

    def call(segs, width, *args, **kw):
        names = (segs,) if isinstance(segs, str) else segs
        assert len({col[n][0] for n in names}) == 1
        ranges = tuple((col[n][1], width) for n in names)
        return _proj_call(h, groups[col[names[0]][0]], l, ranges[0] if len(names) == 1 else ranges, *args, **kw)

    res = {}
    res['qk'] = call(
        'aq', 2 * wq, _ep_rope, [rope_tab],
        [pl.BlockSpec((None, 2, tm, A_DK), lambda b, i, j: (j, 0, i, 0))],
        jax.ShapeDtypeStruct((b_, t_, 2 * wq), BF16), row_spec, tm=tm, tn=tn, name="proj_qk_rope")
    res['av'] = call(
        'av', wv, _ep_plain, [], [],
        jax.ShapeDtypeStruct((b_, t_, wv), F32), row_spec, tm=tm, tn=tn, name="proj_av")
    res['og'] = call(
        ('ao', 'az'), wv, _ep_outgate, [], [],
        jax.ShapeDtypeStruct((b_, t_, wv), F32),
        pl.BlockSpec((1, tm, tn // 2), lambda b, i, j: (b, i, j)), tm=tm, tn=tn // 2, name="proj_outgate")
    res['gates'] = call(
        'gates', 2 * LANES, _ep_plain, [], [],
        jax.ShapeDtypeStruct((b_, t_, 2 * LANES), F32),
        pl.BlockSpec((1, tm, 2 * LANES), lambda b, i, j: (b, i, j)), tm=tm, tn=2 * LANES, name="proj_gates")
    res['bqk'] = call(
        'bq', 2 * wh, _ep_headnorm, [gain],
        [pl.BlockSpec((None, 1, tn), lambda b, i, j: (l, 0, j))],
        jax.ShapeDtypeStruct((b_, 2 * B_HEADS, t_, B_DH), BF16), hm_spec, tm=tm, tn=tn, name="proj_bqk_norm")
    res['bv'] = call(
        'bv', wh, _ep_headmajor, [], [],
        jax.ShapeDtypeStruct((b_, B_HEADS, t_, B_DH), BF16), hm_spec, tm=tm, tn=tn, name="proj_bv")
    res['bz'] = call(
        'bz', wh, _ep_headmajor_silu, [], [],
        jax.ShapeDtypeStruct((b_, B_HEADS, t_, B_DH), F32), hm_spec, tm=tm, tn=tn, name="proj_bz_silu")
    res['gab'] = call(
        'ga', 2 * d, _ep_sigmoid, [], [],
        jax.ShapeDtypeStruct((b_, t_, 2 * d), F32), row_spec, tm=tm, tn=tn, name="proj_gab_sigmoid")
    return res


def kernel(x, c, ctx, c_ctx, w_mod, b_mod, norm_g, w_in, b_in, a_norm_g, w_br_a, w_br_b, na_q_g, na_k_g, na_rpb, w_out, b_out):
    bsz, n_tok, d = x.shape
    n_ctx = ctx.shape[1]
    depth = w_mod.shape[0]
    assert n_tok % MLSTM_CHUNK == 0 and n_ctx % MLSTM_CHUNK == 0 and n_tok % (NA_QROWS * GRID_W) == 0
    mod_rows = 16
    assert bsz + 1 <= mod_rows
    tm_lat = min(1024, n_tok)
    tm_ctx = min(1024, n_ctx)
    te_lat = min(512, n_tok)
    te_ctx = min(512, n_ctx)

    rope_lat = _rope_tables(n_tok, True)
    rope_ctx = _rope_tables(n_ctx, False)
    w_in_b, col, qk_gain = _prep_in_weights(w_in, b_in, na_q_g, na_k_g)
    bias = _na_column_tables(na_rpb)
    w_mod_b = w_mod.astype(BF16)
    b_mod_r = b_mod[:, None, :]
    wa = w_br_a.astype(BF16)
    wb = w_br_b.astype(BF16)
    wo = w_out.astype(BF16)
    bo = b_out[:, None, :]
    a_g = a_norm_g[:, None, :]
    c_rows = jnp.zeros((mod_rows, d), F32).at[:bsz].set(c).at[bsz].set(c_ctx)
    zero_state = (jnp.zeros((bsz, 2 * A_HEADS, A_DK, A_DV), F32),
                  jnp.zeros((bsz, 2 * A_HEADS, 1, A_DK), F32),
                  jnp.zeros((bsz, 1, LANES), F32))

    def merge(ya, yb, sgab, x_in, gate, l, tm):
        u = _merge1_call(ya, yb, wa, wb, sgab, l, tm=tm, tn=512)
        return _merge2_call(u, wo, bo, x_in, gate, l, tm=tm, tn=1024)

    for l in range(depth):
        need_ctx = l < depth - 1
        mod = _mod_call(c_rows, w_mod_b, b_mod_r, l)
        shift, scale, gate = (mod[:bsz, k * d:(k + 1) * d].reshape(bsz, 1, d) for k in range(3))
        shift_c, scale_c, gate_c = (jnp.broadcast_to(mod[bsz, k * d:(k + 1) * d].reshape(1, 1, d), (bsz, 1, d))
                                    for k in range(3))
        g_row = norm_g[l].reshape(1, d)
        h_lat = _modulate_call(x, g_row, shift, scale, te_lat)
        h_ctx = _modulate_call(ctx, g_row, shift_c, scale_c, te_ctx)
        pl_ = _project(h_lat, w_in_b, col, qk_gain, l, rope_lat, tm=tm_lat)
        pc_ = _project(h_ctx, w_in_b, col, qk_gain, l, rope_ctx, tm=tm_ctx)

        hcf, hcb, c1, n1, m1 = _mlstm_call(pc_['qk'], pc_['av'], pc_['gates'], zero_state,
                                           chunk=MLSTM_CHUNK, dirs=(0, 1), name="mlstm_ctx")
        hlf = _mlstm_call(pl_['qk'], pl_['av'], pl_['gates'], (c1, n1, m1),
                          chunk=MLSTM_CHUNK, dirs=(0,), name="mlstm_lat_fwd")[0]
        ya_l = _mlstm_call(pl_['qk'], pl_['av'], pl_['gates'], (c1, n1, m1), chunk=MLSTM_CHUNK, dirs=(1,),
                           name="mlstm_lat_bwd", fuse=(hlf, pl_['og'], a_g, l))[0]

        yb_l = _na_call(pl_['bqk'], pl_['bv'], pc_['bqk'], pc_['bv'], pl_['bz'], bias, l)

        x = merge(ya_l, yb_l, pl_['gab'], x, gate, l, tm_lat)
        if need_ctx:
            ya_c = _mlstm_out_call(hcf, hcb, pc_['og'], a_g, l, te_ctx)
            yb_c = _ctx_attn_call(pc_['bqk'], pc_['bv'], pc_['bz'])
            ctx = merge(ya_c, yb_c, pc_['gab'], ctx, gate_c, l, tm_ctx)
    return x
```
